```python
import math
import jax
import jax.numpy as jnp
from jax import lax
import numpy as np

D_MODEL = 1024
BATCH = 8
SEQ = 2048
DEPTH = 4
DEC_BATCH = 128
DEC_SEQ = 1
PAST_LEN = 2048
PAGE_SIZE = 128

GDN_HEADS = 4
GDN_DK = 128
GDN_DV = 128
GDN_CONV = 4
GDN_CHUNK = 64
MOBA_HEADS = 8
MOBA_HD = 64
MOBA_BLOCK = 256
MOBA_TOPK = 3
MOBA_QCHUNK = 32
REL_BUCKETS = 32
REL_MAX_DIST = 128
N_EXPERTS = 32
N_GROUPS = 4
EXPERTS_PER_GROUP = N_EXPERTS // N_GROUPS
TOP_K = 2
D_FF_EXPERT = 512
EXPERT_BLOCK = 128
PLE_DIM = 256
DN_ALPHA = (2 * DEPTH) ** 0.25
DN_BETA = (8 * DEPTH) ** -0.25
LN_EPS = 1e-5
NORM_EPS = 1e-6

GDN_QK = GDN_HEADS * GDN_DK
GDN_VW = GDN_HEADS * GDN_DV
GDN_CONV_CH = 2 * GDN_QK + GDN_VW
MOBA_W = MOBA_HEADS * MOBA_HD
COL_Z = GDN_CONV_CH
COL_A = COL_Z + GDN_VW
COL_B = COL_A + GDN_HEADS
COL_MOBA = COL_B + GDN_HEADS
COL_GATE = COL_MOBA + 3 * MOBA_W
N_IN = COL_GATE + 2 * D_MODEL

kernel_name = 'hybrid_gdn_moba_moe_decoder_step'


def layer_norm(x, g, b):
    xf = x.astype(jnp.float32)
    mu = jnp.mean(xf, -1, keepdims=True)
    xc = xf - mu
    var = jnp.mean(xc * xc, -1, keepdims=True)
    return (xc * lax.rsqrt(var + LN_EPS) * g + b).astype(x.dtype)


def l2_normalize(x):
    xf = x.astype(jnp.float32)
    return xf * lax.rsqrt(jnp.sum(xf * xf, -1, keepdims=True) + NORM_EPS)


def t5_bucket(dist):
    n = jnp.maximum(dist, 0)
    exact = REL_BUCKETS // 2
    log_ratio = jnp.log(jnp.maximum(n, 1).astype(jnp.float32) / exact) / math.log(REL_MAX_DIST / exact)
    large = exact + (log_ratio * (REL_BUCKETS - exact)).astype(jnp.int32)
    return jnp.where(n < exact, n, jnp.minimum(large, REL_BUCKETS - 1))


def causal_short_conv(x, prev, w):
    T = x.shape[1]
    xp = jnp.concatenate([prev.astype(x.dtype), x], axis=1)
    y = xp[:, 0:T] * w[0]
    for j in range(1, GDN_CONV):
        y = y + xp[:, j:j + T] * w[j]
    return jax.nn.silu(y), xp[:, T:]


def gated_delta_rule(q, k, v, g, beta, s0):
    N, T, H, _ = k.shape
    C = GDN_CHUNK if T >= GDN_CHUNK else T
    nc = -(-T // C)
    pad = nc * C - T

    def blocks(a):
        a = a.astype(jnp.float32)
        a = jnp.pad(a, [(0, 0), (0, pad)] + [(0, 0)] * (a.ndim - 2))
        a = a.reshape((N, nc, C) + a.shape[2:])
        return jnp.moveaxis(a, (1, 3), (0, 2))

    qb, kb, vb, gb, bb = blocks(q), blocks(k), blocks(v), blocks(g), blocks(beta)
    gc = jnp.cumsum(gb, axis=-1)
    incl = jnp.tril(jnp.ones((C, C), bool))
    strict = jnp.tril(jnp.ones((C, C), bool), -1)
    diff = gc[..., :, None] - gc[..., None, :]
    decay = jnp.where(incl, jnp.exp(jnp.where(incl, diff, 0.0)), 0.0)
    k_beta = kb * bb[..., None]
    lower = jnp.where(strict, jnp.einsum('xnhid,xnhjd->xnhij', k_beta, kb) * decay, 0.0)
    eye = jnp.eye(C, dtype=jnp.float32)
    t_inv = lax.linalg.triangular_solve(eye + lower, jnp.broadcast_to(eye, lower.shape),
                                        left_side=True, lower=True, unit_diagonal=True)
    u = jnp.einsum('xnhij,xnhje->xnhie', t_inv, vb * bb[..., None])
    w = jnp.einsum('xnhij,xnhjd->xnhid', t_inv, k_beta * jnp.exp(gc)[..., None])
    intra = jnp.where(incl, jnp.einsum('xnhid,xnhjd->xnhij', qb, kb) * decay, 0.0)

    def step(S, xs):
        q_c, k_c, u_c, w_c, gc_c, a_c = xs
        v_new = u_c - jnp.einsum('nhcd,nhde->nhce', w_c, S)
        o = (jnp.einsum('nhcd,nhde->nhce', q_c * jnp.exp(gc_c)[..., None], S)
             + jnp.einsum('nhij,nhje->nhie', a_c, v_new))
        g_last = gc_c[..., -1]
        S = (S * jnp.exp(g_last)[..., None, None]
             + jnp.einsum('nhcd,nhce->nhde', k_c * jnp.exp(g_last[..., None] - gc_c)[..., None], v_new))
        return S, o

    s_fin, o = lax.scan(step, s0.astype(jnp.float32), (qb, kb, u, w, gc, intra))
    o = jnp.transpose(o, (1, 0, 3, 2, 4)).reshape(N, nc * C, H, -1)[:, :T]
    return o, s_fin


def gdn_branch(proj, conv_w, a_log, dt_bias, norm_g, conv_prev, s_prev):
    N, T, _ = proj.shape
    qkv, conv_new = causal_short_conv(proj[..., :GDN_CONV_CH], conv_prev, conv_w)
    q = l2_normalize(qkv[..., :GDN_QK].reshape(N, T, GDN_HEADS, GDN_DK)) * (GDN_DK ** -0.5)
    k = l2_normalize(qkv[..., GDN_QK:2 * GDN_QK].reshape(N, T, GDN_HEADS, GDN_DK))
    v = qkv[..., 2 * GDN_QK:].reshape(N, T, GDN_HEADS, GDN_DV)
    z = proj[..., COL_Z:COL_A].reshape(N, T, GDN_HEADS, GDN_DV).astype(jnp.float32)
    a = proj[..., COL_A:COL_B].astype(jnp.float32)
    b = proj[..., COL_B:COL_MOBA].astype(jnp.float32)
    g = -jnp.exp(a_log.astype(jnp.float32)) * jax.nn.softplus(a + dt_bias)
    beta = jax.nn.sigmoid(b)
    o, s_new = gated_delta_rule(q, k, v, g, beta, s_prev)
    o = o * lax.rsqrt(jnp.mean(o * o, -1, keepdims=True) + NORM_EPS) * norm_g * jax.nn.silu(z)
    return o.reshape(N, T, GDN_VW).astype(proj.dtype), conv_new, s_new


def moba_attention(q, k_all, v_all, q_pos, rel_bias):
    N, Tq, H, d = q.shape
    L = k_all.shape[1]
    nb = -(-L // MOBA_BLOCK)
    pad = nb * MOBA_BLOCK - L

    def to_blocks(a):
        a = jnp.pad(a, ((0, 0), (0, pad), (0, 0), (0, 0))).reshape(N, nb, MOBA_BLOCK, H, d)
        return a.transpose(0, 3, 1, 2, 4)

    kb = to_blocks(k_all)
    vb = to_blocks(v_all)
    k_mean = jnp.mean(kb.astype(jnp.float32), axis=3)
    n_sel = min(MOBA_TOPK, nb)
    c_len = min(MOBA_QCHUNK, Tq)
    n_qc = -(-Tq // c_len)
    qpad = n_qc * c_len - Tq
    q_p = jnp.pad(q, ((0, 0), (0, qpad), (0, 0), (0, 0)))
    pos_p = jnp.concatenate([q_pos, jnp.full((qpad,), q_pos[-1], q_pos.dtype)])
    q_chunks = q_p.reshape(N, n_qc, c_len, H, d).transpose(1, 0, 2, 3, 4)
    pos_chunks = pos_p.reshape(n_qc, c_len)
    n_ix = jnp.arange(N)[:, None, None, None]
    h_ix = jnp.arange(H)[None, :, None, None]
    h_ix5 = h_ix[..., None]
    offs = jnp.arange(MOBA_BLOCK)
    bias_t = rel_bias.T.astype(jnp.float32)
    scale = d ** -0.5

    def attend(args):
        qc, pc = args
        qf = qc.astype(jnp.float32)
        own = pc // MOBA_BLOCK
        past = jnp.arange(nb)[None, :] < own[:, None]
        gate = jnp.einsum('nchd,nhbd->nhcb', qf, k_mean)
        gate = jnp.where(past, gate, -jnp.inf)
        _, sel = lax.top_k(gate, n_sel)
        idx = jnp.concatenate([sel, jnp.broadcast_to(own[None, None, :, None], (N, H, c_len, 1))], axis=-1)
        slot_ok = jnp.concatenate([jnp.arange(n_sel)[None, :] < own[:, None],
                                   jnp.ones((c_len, 1), bool)], axis=-1)
        kg = kb[n_ix, h_ix, idx]
        vg = vb[n_ix, h_ix, idx]
        dist = pc[:, None, None] - (idx[..., None] * MOBA_BLOCK + offs)
        valid = slot_ok[:, :, None] & (dist >= 0)
        bias = bias_t[h_ix5, t5_bucket(dist)]
        logits = jnp.einsum('nchd,nhcskd->nhcsk', qf, kg) * scale + bias
        logits = jnp.where(valid, logits, -jnp.inf)
        wts = jax.nn.softmax(logits.reshape(N, H, c_len, -1), axis=-1).reshape(logits.shape)
        return jnp.einsum('nhcsk,nhcskd->nchd', wts, vg)

    out = lax.map(attend, (q_chunks, pos_chunks))
    out = out.transpose(1, 0, 2, 3, 4).reshape(N, n_qc * c_len, H, d)[:, :Tq]
    return out.astype(q.dtype)


def moba_branch(proj, rel_bias, past_k, past_v, q_pos):
    N, T, _ = proj.shape
    base = COL_MOBA
    q = proj[..., base:base + MOBA_W].reshape(N, T, MOBA_HEADS, MOBA_HD)
    k = proj[..., base + MOBA_W:base + 2 * MOBA_W].reshape(N, T, MOBA_HEADS, MOBA_HD)
    v = proj[..., base + 2 * MOBA_W:base + 3 * MOBA_W].reshape(N, T, MOBA_HEADS, MOBA_HD)
    if past_k is None:
        k_all, v_all = k, v
    else:
        k_all = jnp.concatenate([past_k.astype(k.dtype), k], axis=1)
        v_all = jnp.concatenate([past_v.astype(v.dtype), v], axis=1)
    o = moba_attention(q, k_all, v_all, q_pos, rel_bias)
    return o.reshape(N, T, MOBA_W), k, v


def route(x, w_router, b_router):
    aff = jax.nn.sigmoid(x.astype(jnp.float32) @ w_router.astype(jnp.float32))
    sel = (aff + b_router).reshape(-1, N_GROUPS, EXPERTS_PER_GROUP)
    grp = jnp.argmax(jnp.sum(lax.top_k(sel, TOP_K)[0], -1), -1)
    in_grp = jnp.take_along_axis(sel, grp[:, None, None], axis=1)[:, 0]
    _, local = lax.top_k(in_grp, TOP_K)
    expert = grp[:, None] * EXPERTS_PER_GROUP + local
    wts = jnp.take_along_axis(aff, expert, -1)
    return expert, wts / jnp.sum(wts, -1, keepdims=True)


def moe_ffn(h, w_router, b_router, w_gate, w_up, w_down):
    N, T, D = h.shape
    x = h.reshape(-1, D)
    expert, wts = route(x, w_router, b_router)
    flat_e = expert.reshape(-1)
    n_asg = flat_e.shape[0]
    order = jnp.argsort(flat_e)
    e_sorted = flat_e[order]
    tok = order // TOP_K
    counts = jnp.bincount(flat_e, length=N_EXPERTS)
    padded = (counts + EXPERT_BLOCK - 1) // EXPERT_BLOCK * EXPERT_BLOCK
    start = jnp.cumsum(counts) - counts
    ends = jnp.cumsum(padded)
    pstart = ends - padded
    dest = pstart[e_sorted] + jnp.arange(n_asg) - start[e_sorted]
    n_blocks = -(-(n_asg + N_EXPERTS * (EXPERT_BLOCK - 1)) // EXPERT_BLOCK)
    xs = jnp.zeros((n_blocks * EXPERT_BLOCK, D), x.dtype).at[dest].set(x[tok])
    block_e = jnp.minimum(jnp.searchsorted(ends, jnp.arange(n_blocks) * EXPERT_BLOCK, side='right'), N_EXPERTS - 1)

    def expert_block(args):
        xb, e = args
        hid = jax.nn.silu(xb @ w_gate[e]) * (xb @ w_up[e])
        return hid @ w_down[e]

    ys = lax.map(expert_block, (xs.reshape(n_blocks, EXPERT_BLOCK, D), block_e)).reshape(-1, D)
    contrib = ys[dest] * wts.reshape(-1)[order][:, None].astype(ys.dtype)
    return jnp.zeros_like(x).at[tok].add(contrib).reshape(N, T, D)


def gather_pages(pool, page_table):
    g = pool[page_table]
    return g.reshape(g.shape[0], -1, g.shape[3], g.shape[4])


def run_trunk(x, p, q_pos, conv_prev, s_prev, cache_k, cache_v, page_table, W):
    x = layer_norm(x, W['ln0_g'], W['ln0_b'])
    N = x.shape[0]
    ks, vs, ss, cs = [], [], [], []
    for i in range(DEPTH):
        proj = x @ W['w_in'][i]
        if conv_prev is None:
            c0 = jnp.zeros((N, GDN_CONV - 1, GDN_CONV_CH), x.dtype)
            s0 = jnp.zeros((N, GDN_HEADS, GDN_DK, GDN_DV), jnp.float32)
            pk = None
            pv = None
        else:
            c0 = conv_prev[i]
            s0 = s_prev[i]
            pk = gather_pages(cache_k[i], page_table)
            pv = gather_pages(cache_v[i], page_table)
        y_a, c1, s1 = gdn_branch(proj, W['gdn_conv_w'][i], W['gdn_a_log'][i], W['gdn_dt_bias'][i],
                                 W['gdn_norm_g'][i], c0, s0)
        y_b, k1, v1 = moba_branch(proj, W['rel_bias'], pk, pv, q_pos)
        gate = jax.nn.sigmoid(proj[..., COL_GATE:])
        merged = (gate[..., :D_MODEL] * (y_a @ W['w_up_gdn'][i])
                  + gate[..., D_MODEL:] * (y_b @ W['w_up_moba'][i]))
        h = layer_norm(DN_ALPHA * x + merged @ W['w_o'][i], W['ln1_g'][i], W['ln1_b'][i])
        ffn = moe_ffn(h, W['w_router'], W['b_router'], W['w_exp_gate'][i], W['w_exp_up'][i], W['w_exp_down'][i])
        ple = jax.nn.sigmoid(h @ W['w_ple_gate'][i]) * (p[i] @ W['w_ple_proj'][i])
        x = layer_norm(DN_ALPHA * h + ffn + ple, W['ln2_g'][i], W['ln2_b'][i])
        ks.append(k1)
        vs.append(v1)
        ss.append(s1)
        cs.append(c1)
    return x, jnp.stack(ks), jnp.stack(vs), jnp.stack(ss), jnp.stack(cs)


def setup_inputs(seed: int = 0) -> dict:
    key = jax.random.key(seed)
    ks = iter(jax.random.split(key, 48))
    f32 = jnp.float32

    def nrm(shape, scale):
        return jax.random.normal(next(ks), shape, f32) * scale

    n_pages = PAST_LEN // PAGE_SIZE
    n_phys = (DEC_BATCH * n_pages * 5) // 4
    perm = jax.random.permutation(next(ks), n_phys)
    page_table = perm[:DEC_BATCH * n_pages].reshape(DEC_BATCH, n_pages).astype(jnp.int32)
    dt = jnp.exp(jax.random.uniform(next(ks), (DEPTH, GDN_HEADS), f32, math.log(1e-3), math.log(1e-1)))
    return {
        'x_prompt': nrm((BATCH, SEQ, D_MODEL), 1.0),
        'x_sample': nrm((DEC_BATCH, DEC_SEQ, D_MODEL), 1.0),
        'cache_k': nrm((DEPTH, n_phys, PAGE_SIZE, MOBA_HEADS, MOBA_HD), 1.0),
        'cache_v': nrm((DEPTH, n_phys, PAGE_SIZE, MOBA_HEADS, MOBA_HD), 1.0),
        'state_gdn': nrm((DEPTH, DEC_BATCH, GDN_HEADS, GDN_DK, GDN_DV), 0.1),
        'state_conv': nrm((DEPTH, DEC_BATCH, GDN_CONV - 1, GDN_CONV_CH), 1.0),
        'page_table': page_table,
        'p_prompt': nrm((DEPTH, BATCH, SEQ, PLE_DIM), 1.0),
        'p_sample': nrm((DEPTH, DEC_BATCH, DEC_SEQ, PLE_DIM), 1.0),
        'ln0_g': 1.0 + nrm((D_MODEL,), 0.02),
        'ln0_b': nrm((D_MODEL,), 0.02),
        'w_in': nrm((DEPTH, D_MODEL, N_IN), D_MODEL ** -0.5),
        'gdn_conv_w': nrm((DEPTH, GDN_CONV, GDN_CONV_CH), GDN_CONV ** -0.5),
        'gdn_a_log': jnp.log(jax.random.uniform(next(ks), (DEPTH, GDN_HEADS), f32, 1.0, 16.0)),
        'gdn_dt_bias': dt + jnp.log(-jnp.expm1(-dt)),
        'gdn_norm_g': 1.0 + nrm((DEPTH, GDN_DV), 0.02),
        'w_up_gdn': nrm((DEPTH, GDN_VW, D_MODEL), GDN_VW ** -0.5),
        'w_up_moba': nrm((DEPTH, MOBA_W, D_MODEL), MOBA_W ** -0.5),
        'w_o': nrm((DEPTH, D_MODEL, D_MODEL), D_MODEL ** -0.5 * DN_BETA),
        'ln1_g': 1.0 + nrm((DEPTH, D_MODEL), 0.02),
        'ln1_b': nrm((DEPTH, D_MODEL), 0.02),
        'rel_bias': nrm((REL_BUCKETS, MOBA_HEADS), 0.5),
        'w_router': nrm((D_MODEL, N_EXPERTS), D_MODEL ** -0.5),
        'b_router': nrm((N_EXPERTS,), 0.01),
        'w_exp_gate': nrm((DEPTH, N_EXPERTS, D_MODEL, D_FF_EXPERT), D_MODEL ** -0.5),
        'w_exp_up': nrm((DEPTH, N_EXPERTS, D_MODEL, D_FF_EXPERT), D_MODEL ** -0.5),
        'w_exp_down': nrm((DEPTH, N_EXPERTS, D_FF_EXPERT, D_MODEL), D_FF_EXPERT ** -0.5 * DN_BETA),
        'w_ple_gate': nrm((DEPTH, D_MODEL, D_MODEL), D_MODEL ** -0.5),
        'w_ple_proj': nrm((DEPTH, PLE_DIM, D_MODEL), PLE_DIM ** -0.5 * DN_BETA),
        'ln2_g': 1.0 + nrm((DEPTH, D_MODEL), 0.02),
        'ln2_b': nrm((DEPTH, D_MODEL), 0.02),
    }


def reference(x_prompt, x_sample, cache_k, cache_v, state_gdn, state_conv, page_table, p_prompt, p_sample,
              ln0_g, ln0_b, w_in, gdn_conv_w, gdn_a_log, gdn_dt_bias, gdn_norm_g, w_up_gdn, w_up_moba, w_o,
              ln1_g, ln1_b, rel_bias, w_router, b_router, w_exp_gate, w_exp_up, w_exp_down,
              w_ple_gate, w_ple_proj, ln2_g, ln2_b):
    W = {'ln0_g': ln0_g, 'ln0_b': ln0_b, 'w_in': w_in, 'gdn_conv_w': gdn_conv_w, 'gdn_a_log': gdn_a_log,
         'gdn_dt_bias': gdn_dt_bias, 'gdn_norm_g': gdn_norm_g, 'w_up_gdn': w_up_gdn, 'w_up_moba': w_up_moba,
         'w_o': w_o, 'ln1_g': ln1_g, 'ln1_b': ln1_b, 'rel_bias': rel_bias, 'w_router': w_router,
         'b_router': b_router, 'w_exp_gate': w_exp_gate, 'w_exp_up': w_exp_up, 'w_exp_down': w_exp_down,
         'w_ple_gate': w_ple_gate, 'w_ple_proj': w_ple_proj, 'ln2_g': ln2_g, 'ln2_b': ln2_b}
    pos_prompt = jnp.arange(x_prompt.shape[1], dtype=jnp.int32)
    pos_sample = PAST_LEN + jnp.arange(x_sample.shape[1], dtype=jnp.int32)
    y_prompt, k_prompt, v_prompt, gdn_prompt, conv_prompt = run_trunk(
        x_prompt, p_prompt, pos_prompt, None, None, None, None, None, W)
    y_sample, k_sample, v_sample, gdn_sample, conv_sample = run_trunk(
        x_sample, p_sample, pos_sample, state_conv, state_gdn, cache_k, cache_v, page_table, W)
    return (y_prompt, y_sample, k_prompt, v_prompt, gdn_prompt, conv_prompt,
            k_sample, v_sample, gdn_sample, conv_sample)
```

```python
import functools
import math

import numpy as np
import jax
import jax.numpy as jnp
from jax import lax
from jax.experimental import pallas as pl
from jax.experimental.pallas import tpu as pltpu

F32 = jnp.float32
BF16 = jnp.bfloat16
I32 = jnp.int32
HIGHEST = lax.Precision.HIGHEST

D_MODEL = 1024
DEPTH = 4
PAST_LEN = 2048
PAGE_SIZE = 128
GDN_HEADS = 4
GDN_DK = 128
GDN_DV = 128
GDN_CONV = 4
GDN_CHUNK = 64
MOBA_HEADS = 8
MOBA_HD = 64
MOBA_BLOCK = 256
MOBA_TOPK = 3
REL_BUCKETS = 32
REL_MAX_DIST = 128
N_EXPERTS = 32
N_GROUPS = 4
EXPERTS_PER_GROUP = N_EXPERTS // N_GROUPS
TOP_K = 2
D_FF_EXPERT = 512
PLE_DIM = 256
DN_ALPHA = (2 * DEPTH) ** 0.25
LN_EPS = 1e-5
NORM_EPS = 1e-6

GDN_QK = GDN_HEADS * GDN_DK
GDN_VW = GDN_HEADS * GDN_DV
GDN_CONV_CH = 2 * GDN_QK + GDN_VW
MOBA_W = MOBA_HEADS * MOBA_HD
COL_Z = GDN_CONV_CH
COL_A = COL_Z + GDN_VW
COL_B = COL_A + GDN_HEADS
COL_MOBA = COL_B + GDN_HEADS
COL_GATE = COL_MOBA + 3 * MOBA_W
N_IN = COL_GATE + 2 * D_MODEL

LANES = 128
SUBLANES = 8
VMEM_LIMIT_BYTES = 56 * 1024 * 1024

EXPERT_ROWS = 256
GDN_GROUP = 256
NEG_BIG = -1e30


def _cparams(*sem):
    return pltpu.CompilerParams(dimension_semantics=sem, vmem_limit_bytes=VMEM_LIMIT_BYTES)


def _row_tile(m, target):
    t = min(m, target)
    assert m % t == 0, (m, t)
    return t


def _t5_thresholds():
    exact = REL_BUCKETS // 2
    n = np.arange(0, 2 * REL_MAX_DIST)
    log_ratio = np.log(np.maximum(n, 1).astype(np.float32) / np.float32(exact)) / np.float32(
        math.log(REL_MAX_DIST / exact))
    large = exact + (log_ratio * np.float32(REL_BUCKETS - exact)).astype(np.int32)
    bucket = np.where(n < exact, n, np.minimum(large, REL_BUCKETS - 1))
    assert np.all(np.diff(bucket) >= 0)
    return [int(np.argmax(bucket >= k)) for k in range(REL_BUCKETS)]


T5_THRESHOLDS = _t5_thresholds()


def _layer_norm(x, g, b):
    mu = jnp.mean(x, axis=-1, keepdims=True)
    xc = x - mu
    var = jnp.mean(xc * xc, axis=-1, keepdims=True)
    return xc * lax.rsqrt(var + LN_EPS) * g + b


def _sigmoid(x):
    return 1.0 / (1.0 + jnp.exp(-x))


def _silu(x):
    return x * _sigmoid(x)


def _softplus(x):
    return jnp.maximum(x, 0.0) + jnp.log(1.0 + jnp.exp(-jnp.abs(x)))


def _dot(a, b):
    return jnp.dot(a.astype(BF16), b.astype(BF16), preferred_element_type=F32)


def _dot_nt(a, b, precision=None):
    if precision is None:
        a, b = a.astype(BF16), b.astype(BF16)
    return lax.dot_general(a, b, (((1,), (1,)), ((), ())), precision=precision,
                           preferred_element_type=F32)


def _dot_tn(a, b, precision=None):
    if precision is None:
        a, b = a.astype(BF16), b.astype(BF16)
    return lax.dot_general(a, b, (((0,), (0,)), ((), ())), precision=precision,
                           preferred_element_type=F32)


def _ln_kernel(x_ref, g_ref, b_ref, o_ref):
    o_ref[...] = _layer_norm(x_ref[...], g_ref[...], b_ref[...])


def _ln_call(x, g, b):
    m, d = x.shape
    tm = _row_tile(m, 512)
    return pl.pallas_call(
        _ln_kernel,
        out_shape=jax.ShapeDtypeStruct((m, d), F32),
        grid=(m // tm,),
        in_specs=[pl.BlockSpec((tm, d), lambda i: (i, 0)),
                  pl.BlockSpec((1, d), lambda i: (0, 0)),
                  pl.BlockSpec((1, d), lambda i: (0, 0))],
        out_specs=pl.BlockSpec((tm, d), lambda i: (i, 0)),
        compiler_params=_cparams("parallel"),
        name="ln0",
    )(x, g.reshape(1, d), b.reshape(1, d))


def _proj_kernel(x_ref, wc_ref, wz_ref, wab_ref, wq_ref, wk_ref, wv_ref, wg_ref,
                 conv_ref, z_ref, ab_ref, q_ref, k_ref, v_ref, gate_ref):
    x = x_ref[...].astype(BF16)
    for w_ref, o_ref in ((wc_ref, conv_ref), (wz_ref, z_ref), (wab_ref, ab_ref), (wq_ref, q_ref),
                         (wk_ref, k_ref), (wv_ref, v_ref), (wg_ref, gate_ref)):
        o_ref[...] = jnp.dot(x, w_ref[...], preferred_element_type=F32)


def _split_w_in(w):
    wab = jnp.pad(w[:, COL_A:COL_MOBA], ((0, 0), (0, LANES - 2 * GDN_HEADS)))
    parts = (w[:, :COL_Z], w[:, COL_Z:COL_A], wab,
             w[:, COL_MOBA:COL_MOBA + MOBA_W], w[:, COL_MOBA + MOBA_W:COL_MOBA + 2 * MOBA_W],
             w[:, COL_MOBA + 2 * MOBA_W:COL_GATE], w[:, COL_GATE:])
    return tuple(p.astype(BF16) for p in parts)


def _proj_call(x, w_parts):
    m, d = x.shape
    tm = _row_tile(m, 256)
    widths = [p.shape[1] for p in w_parts]
    return pl.pallas_call(
        _proj_kernel,
        out_shape=tuple(jax.ShapeDtypeStruct((m, n), F32) for n in widths),
        grid=(m // tm,),
        in_specs=[pl.BlockSpec((tm, d), lambda i: (i, 0))]
        + [pl.BlockSpec((d, n), lambda i: (0, 0)) for n in widths],
        out_specs=tuple(pl.BlockSpec((tm, n), lambda i: (i, 0)) for n in widths),
        compiler_params=_cparams("parallel"),
        name="proj",
    )(x, *w_parts)


def _chunk_cumsum(x, pos_in_chunk, axis):
    s = 1
    while s < GDN_CHUNK:
        x = x + jnp.where(pos_in_chunk >= s, pltpu.roll(x, s, axis), 0.0)
        s *= 2
    return x


def _l2_normalize(x):
    return x * lax.rsqrt(jnp.sum(x * x, axis=-1, keepdims=True) + NORM_EPS)


def _gdn_gates(ab, alog_row, dtb_row):
    g = -jnp.exp(alog_row) * _softplus(ab + dtb_row)
    return g, _sigmoid(ab)


def _gdn_prompt_kernel(qp_ref, kp_ref, vp_ref, z_ref, ab_ref, wq_ref, wk_ref, wv_ref,
                       alog_ref, dtb_ref, ng_ref, y_ref, s_ref,
                       qn_s, kn_s, kb_s, vb_s, gcb_s, gcr_s, qg_s, kg_s, u_s, w_s, intra_s, egl_s):
    h = pl.program_id(1)
    t_len = qp_ref.shape[1]
    n_groups = t_len // GDN_GROUP
    n_chunks = t_len // GDN_CHUNK
    cpg = GDN_GROUP // GDN_CHUNK

    row = lax.broadcasted_iota(I32, (t_len, LANES), 0)
    lane = lax.broadcasted_iota(I32, (t_len, LANES), 1)

    def conv(x_ref, w_ref):
        x = x_ref[0]
        w = w_ref[...]
        y = x * w[GDN_CONV - 1:GDN_CONV]
        for j in range(1, GDN_CONV):
            y = y + jnp.where(row >= j, pltpu.roll(x, j, 0), 0.0) * w[GDN_CONV - 1 - j:GDN_CONV - j]
        return _silu(y)

    qn = _l2_normalize(conv(qp_ref, wq_ref)) * (GDN_DK ** -0.5)
    kn = _l2_normalize(conv(kp_ref, wk_ref))
    vv = conv(vp_ref, wv_ref)

    g_all, beta_all = _gdn_gates(ab_ref[0], alog_ref[...], dtb_ref[...])
    gc_all = _chunk_cumsum(g_all, row % GDN_CHUNK, 0)
    gc_col = jnp.sum(jnp.where(lane == h, gc_all, 0.0), axis=1, keepdims=True)
    beta_col = jnp.sum(jnp.where(lane == h + GDN_HEADS, beta_all, 0.0), axis=1, keepdims=True)

    g_rows = g_all.T[0:SUBLANES]
    lane_t = lax.broadcasted_iota(I32, (SUBLANES, t_len), 1)
    sub_t = lax.broadcasted_iota(I32, (SUBLANES, t_len), 0)
    gc_rows = _chunk_cumsum(g_rows, lane_t % GDN_CHUNK, 1)
    gc_row = jnp.sum(jnp.where(sub_t == h, gc_rows, 0.0), axis=0, keepdims=True)

    qn_s[...] = qn
    kn_s[...] = kn
    kb_s[...] = kn * beta_col
    vb_s[...] = vv * beta_col
    gcb_s[...] = jnp.broadcast_to(gc_col, (t_len, LANES))
    for g in range(n_groups):
        gcr_s[g] = jnp.broadcast_to(gc_row[:, g * GDN_GROUP:(g + 1) * GDN_GROUP], (SUBLANES, GDN_GROUP))

    ii = lax.broadcasted_iota(I32, (GDN_GROUP, GDN_GROUP), 0)
    jj = lax.broadcasted_iota(I32, (GDN_GROUP, GDN_GROUP), 1)
    same = (ii // GDN_CHUNK) == (jj // GDN_CHUNK)
    incl = same & (ii >= jj)
    strict = same & (ii > jj)
    eye = (ii == jj).astype(F32)

    def group_body(g, carry):
        r0 = pl.multiple_of(g * GDN_GROUP, GDN_GROUP)
        rows = pl.ds(r0, GDN_GROUP)
        q_g = qn_s[rows, :]
        k_g = kn_s[rows, :]
        kb_g = kb_s[rows, :]
        vb_g = vb_s[rows, :]
        gcb_g = gcb_s[rows, :]
        gcol = gcb_g[:, 0:1]
        grow = gcr_s[g][0:1, :]
        eg = jnp.exp(gcol)
        dm = jnp.where(incl, gcol - grow, 0.0)
        decay = jnp.where(incl, jnp.exp(dm), 0.0)
        a_mat = jnp.where(strict, _dot_nt(kb_g, k_g) * decay, 0.0)
        intra = _dot_nt(q_g, k_g) * decay
        t_inv = eye - a_mat
        x_pow = a_mat
        p = 2
        while p < GDN_CHUNK:
            x_pow = _dot(x_pow, x_pow)
            t_inv = t_inv + _dot(t_inv, x_pow)
            p *= 2
        uw = _dot(t_inv, jnp.concatenate([vb_g, kb_g * eg], axis=1))
        u_s[rows, :] = uw[:, :GDN_DV]
        w_s[rows, :] = uw[:, GDN_DV:]
        qg_s[rows, :] = q_g * eg
        for c in range(cpg):
            lo = c * GDN_CHUNK
            g_last = gcb_g[lo + GDN_CHUNK - 1:lo + GDN_CHUNK, :]
            rows_c = pl.ds(pl.multiple_of(r0 + lo, GDN_CHUNK), GDN_CHUNK)
            kg_s[rows_c, :] = k_g[lo:lo + GDN_CHUNK] * jnp.exp(g_last - gcb_g[lo:lo + GDN_CHUNK])
            intra_s[rows_c, :] = intra[lo:lo + GDN_CHUNK, lo:lo + GDN_CHUNK]
            egl_s[g * cpg + c] = jnp.broadcast_to(jnp.exp(g_last), (SUBLANES, LANES))
        return carry

    lax.fori_loop(0, n_groups, group_body, 0)

    ng = ng_ref[...]

    def chunk_body(c, s_mat):
        r0 = pl.multiple_of(c * GDN_CHUNK, GDN_CHUNK)
        rows = pl.ds(r0, GDN_CHUNK)
        wq = jnp.concatenate([w_s[rows, :], qg_s[rows, :]], axis=0)
        ws_qs = _dot(wq, s_mat)
        v_new = u_s[rows, :] - ws_qs[:GDN_CHUNK]
        o = ws_qs[GDN_CHUNK:] + _dot(intra_s[rows, :], v_new)
        s_new = s_mat * egl_s[c][0:1, :] + _dot_tn(kg_s[rows, :], v_new)
        o = o * lax.rsqrt(jnp.mean(o * o, axis=-1, keepdims=True) + NORM_EPS) * ng * _silu(z_ref[0, rows, :])
        y_ref[0, rows, :] = o.astype(y_ref.dtype)
        return s_new

    s_fin = lax.fori_loop(0, n_chunks, chunk_body, jnp.zeros((GDN_DK, GDN_DV), F32))
    s_ref[0, 0] = s_fin


def _gdn_prompt_call(conv, z, ab, conv_w, a_log, dt_bias, norm_g):
    n, t_len, _ = conv.shape
    hq = GDN_QK // LANES
    alog_row = jnp.pad(a_log, (0, LANES - GDN_HEADS)).reshape(1, LANES)
    dtb_row = jnp.pad(dt_bias, (0, LANES - GDN_HEADS)).reshape(1, LANES)
    col = lambda off: pl.BlockSpec((1, t_len, LANES), lambda i, h: (i, 0, h + off))
    wcol = lambda off: pl.BlockSpec((GDN_CONV, LANES), lambda i, h: (0, h + off))
    row_spec = pl.BlockSpec((1, LANES), lambda i, h: (0, 0))
    big = pltpu.VMEM((t_len, LANES), F32)
    return pl.pallas_call(
        _gdn_prompt_kernel,
        out_shape=(jax.ShapeDtypeStruct((n, t_len, GDN_VW), BF16),
                   jax.ShapeDtypeStruct((n, GDN_HEADS, GDN_DK, GDN_DV), F32)),
        grid=(n, GDN_HEADS),
        in_specs=[col(0), col(hq), col(2 * hq),
                  pl.BlockSpec((1, t_len, LANES), lambda i, h: (i, 0, h)),
                  pl.BlockSpec((1, t_len, LANES), lambda i, h: (i, 0, 0)),
                  wcol(0), wcol(hq), wcol(2 * hq), row_spec, row_spec, row_spec],
        out_specs=(pl.BlockSpec((1, t_len, LANES), lambda i, h: (i, 0, h)),
                   pl.BlockSpec((1, 1, GDN_DK, GDN_DV), lambda i, h: (i, h, 0, 0))),
        scratch_shapes=[big, big, big, big, big,
                        pltpu.VMEM((t_len // GDN_GROUP, SUBLANES, GDN_GROUP), F32),
                        big, big, big, big,
                        pltpu.VMEM((t_len, GDN_CHUNK), F32),
                        pltpu.VMEM((t_len // GDN_CHUNK, SUBLANES, LANES), F32)],
        compiler_params=_cparams("parallel", "arbitrary"),
        name="gdn_prompt",
    )(conv, conv, conv, z, ab, conv_w, conv_w, conv_w, alog_row, dtb_row, norm_g.reshape(1, GDN_DV))


HEADS_PER_TILE = LANES // MOBA_HD


def _t5_bias(dist, rb_ref, head):
    b = jnp.full(dist.shape, rb_ref[head, 0], F32)
    for k in range(1, REL_BUCKETS):
        b = jnp.where(dist >= T5_THRESHOLDS[k], rb_ref[head, k], b)
    return b


def _top_blocks(gate, n_valid, lane, n_blocks):
    g = jnp.where(lane < n_valid, gate, -jnp.inf)
    rank = jnp.zeros(g.shape, F32)
    for b2 in range(n_blocks):
        col = g[:, b2:b2 + 1]
        rank = rank + jnp.where((col > g) | ((col == g) & (b2 < lane)), 1.0, 0.0)
    return jnp.where((rank < MOBA_TOPK) & (lane < n_valid), 1.0, 0.0)


def _moba_prompt_kernel(rb_ref, q_ref, k_ref, v_ref, o_ref, kmean_s, biasd_s, biasp_s):
    hp = pl.program_id(0)
    n = pl.program_id(1)
    qb = pl.program_id(2)
    blk = MOBA_BLOCK
    n_blocks = k_ref.shape[1] // blk
    ii = lax.broadcasted_iota(I32, (blk, blk), 0)
    jj = lax.broadcasted_iota(I32, (blk, blk), 1)
    lane = lax.broadcasted_iota(I32, (blk, LANES), 1)

    @pl.when((n == 0) & (qb == 0))
    def _():
        for e in range(HEADS_PER_TILE):
            head = hp * HEADS_PER_TILE + e
            biasd_s[e] = _t5_bias(ii - jj, rb_ref, head)
            biasp_s[e] = _t5_bias(blk + ii - jj, rb_ref, head)

    @pl.when(qb == 0)
    def _():
        means = [jnp.mean(k_ref[0, b * blk:(b + 1) * blk, :], axis=0, keepdims=True) for b in range(n_blocks)]
        means.append(jnp.zeros((LANES - n_blocks, LANES), F32))
        kmean_s[...] = jnp.concatenate(means, axis=0)

    q2 = q_ref[0]
    own_rows = pl.ds(pl.multiple_of(qb * blk, blk), blk)
    k_own = k_ref[0, own_rows, :].astype(BF16)
    v_own = v_ref[0, own_rows, :].astype(BF16)
    scale = MOBA_HD ** -0.5

    q_scaled, sels, fars, init = [], [], [], []
    for e in range(HEADS_PER_TILE):
        qh = jnp.where(lane // MOBA_HD == e, q2, 0.0)
        gate = _dot_nt(qh, kmean_s[...], precision=HIGHEST)
        sels.append(_top_blocks(gate, qb, lane, n_blocks))
        fars.append(rb_ref[hp * HEADS_PER_TILE + e, REL_BUCKETS - 1])
        qs = (qh * scale).astype(BF16)
        q_scaled.append(qs)
        s = jnp.where(ii >= jj, _dot_nt(qs, k_own) + biasd_s[e], NEG_BIG)
        m = jnp.max(s, axis=-1, keepdims=True)
        p = jnp.exp(s - m)
        init += [m, jnp.sum(p, axis=-1, keepdims=True), _dot(p, v_own)]

    def past_block(b, carry):
        rows = pl.ds(pl.multiple_of(b * blk, blk), blk)
        k_b = k_ref[0, rows, :].astype(BF16)
        v_b = v_ref[0, rows, :].astype(BF16)
        out = []
        for e in range(HEADS_PER_TILE):
            m, l, acc = carry[3 * e:3 * e + 3]
            bias = jnp.where(b == qb - 1, biasp_s[e], fars[e])
            picked = jnp.sum(jnp.where(lane == b, sels[e], 0.0), axis=-1, keepdims=True)
            s = jnp.where(picked > 0.0, _dot_nt(q_scaled[e], k_b) + bias, NEG_BIG)
            m_new = jnp.maximum(m, jnp.max(s, axis=-1, keepdims=True))
            alpha = jnp.exp(m - m_new)
            p = jnp.exp(s - m_new)
            out += [m_new, alpha * l + jnp.sum(p, axis=-1, keepdims=True), alpha * acc + _dot(p, v_b)]
        return tuple(out)

    fin = lax.fori_loop(0, qb, past_block, tuple(init))
    res = fin[2] / fin[1]
    for e in range(1, HEADS_PER_TILE):
        res = jnp.where(lane // MOBA_HD == e, fin[3 * e + 2] / fin[3 * e + 1], res)
    o_ref[0] = res.astype(o_ref.dtype)


def _moba_prompt_call(q, k, v, rel_bias):
    n, t_len, _ = q.shape
    blk = MOBA_BLOCK
    n_tiles = MOBA_W // LANES
    kv_spec = pl.BlockSpec((1, t_len, LANES), lambda hp, i, qb, rb: (i, 0, hp))
    q_spec = pl.BlockSpec((1, blk, LANES), lambda hp, i, qb, rb: (i, qb, hp))
    return pl.pallas_call(
        _moba_prompt_kernel,
        out_shape=jax.ShapeDtypeStruct((n, t_len, MOBA_W), BF16),
        grid_spec=pltpu.PrefetchScalarGridSpec(
            num_scalar_prefetch=1,
            grid=(n_tiles, n, t_len // blk),
            in_specs=[q_spec, kv_spec, kv_spec],
            out_specs=q_spec,
            scratch_shapes=[pltpu.VMEM((LANES, LANES), F32),
                            pltpu.VMEM((HEADS_PER_TILE, blk, blk), F32),
                            pltpu.VMEM((HEADS_PER_TILE, blk, blk), F32)]),
        compiler_params=_cparams("arbitrary", "arbitrary", "arbitrary"),
        name="moba_prompt",
    )(rel_bias.T, q, k, v)


def _merge_kernel(ya_ref, yb_ref, gate_ref, x_ref, p_ref, wug_ref, wum_ref, wo_ref, wpg_ref, wpp_ref,
                  wrt_ref, g_ref, b_ref, h_ref, ple_ref, aff_ref):
    gate = _sigmoid(gate_ref[...])
    up_a = jnp.dot(ya_ref[...], wug_ref[...], preferred_element_type=F32)
    up_b = jnp.dot(yb_ref[...], wum_ref[...], preferred_element_type=F32)
    merged = gate[:, :D_MODEL] * up_a + gate[:, D_MODEL:] * up_b
    h = _layer_norm(DN_ALPHA * x_ref[...] + _dot(merged, wo_ref[...]), g_ref[...], b_ref[...])
    h_ref[...] = h
    ple_ref[...] = _sigmoid(_dot(h, wpg_ref[...])) * _dot(p_ref[0], wpp_ref[...])
    aff_ref[...] = _sigmoid(_dot_nt(wrt_ref[...], h, precision=HIGHEST))


def _merge_call(ya, yb, gate, x, p, layer, wug, wum, wo, wpg, wpp, wrt, ln_g, ln_b, tm):
    m, d = x.shape
    rows = lambda n: pl.BlockSpec((tm, n), lambda i: (i, 0))
    full = lambda a: pl.BlockSpec(a.shape, lambda i: (0, 0))
    return pl.pallas_call(
        _merge_kernel,
        out_shape=(jax.ShapeDtypeStruct((m, d), F32), jax.ShapeDtypeStruct((m, d), F32),
                   jax.ShapeDtypeStruct((N_EXPERTS, m), F32)),
        grid=(m // tm,),
        in_specs=[rows(GDN_VW), rows(MOBA_W), rows(2 * d), rows(d),
                  pl.BlockSpec((1, tm, PLE_DIM), lambda i: (layer, i, 0)),
                  full(wug), full(wum), full(wo), full(wpg), full(wpp), full(wrt),
                  pl.BlockSpec((1, d), lambda i: (0, 0)), pl.BlockSpec((1, d), lambda i: (0, 0))],
        out_specs=(rows(d), rows(d), pl.BlockSpec((N_EXPERTS, tm), lambda i: (0, i))),
        compiler_params=_cparams("parallel"),
        name="merge",
    )(ya, yb, gate, x, p, wug, wum, wo, wpg, wpp, wrt, ln_g.reshape(1, d), ln_b.reshape(1, d))


def _route_kernel(aff_ref, br_ref, eid_ref, wt_ref, pos_ref, cnt_ref, carry_s, tri_s):
    i = pl.program_id(0)
    tn = aff_ref.shape[1]

    @pl.when(i == 0)
    def _():
        carry_s[...] = jnp.zeros(carry_s.shape, F32)
        r = lax.broadcasted_iota(I32, (tn, tn), 0)
        c = lax.broadcasted_iota(I32, (tn, tn), 1)
        tri_s[...] = jnp.where(r < c, 1.0, 0.0).astype(tri_s.dtype)

    aff = aff_ref[...]
    sel = aff + br_ref[...]
    gsz = EXPERTS_PER_GROUP
    sub = lax.broadcasted_iota(I32, (gsz, tn), 0)
    best = None
    for g in range(N_GROUPS):
        v = sel[g * gsz:(g + 1) * gsz]
        a = aff[g * gsz:(g + 1) * gsz]
        m1 = jnp.max(v, axis=0, keepdims=True)
        i1 = jnp.min(jnp.where(v == m1, sub, gsz), axis=0, keepdims=True)
        v2 = jnp.where(sub == i1, -jnp.inf, v)
        m2 = jnp.max(v2, axis=0, keepdims=True)
        i2 = jnp.min(jnp.where(v2 == m2, sub, gsz), axis=0, keepdims=True)
        a1 = jnp.sum(jnp.where(sub == i1, a, 0.0), axis=0, keepdims=True)
        a2 = jnp.sum(jnp.where(sub == i2, a, 0.0), axis=0, keepdims=True)
        cand = (m1 + m2, i1 + g * gsz, i2 + g * gsz, a1, a2)
        if best is None:
            best = cand
        else:
            better = cand[0] > best[0]
            best = tuple(jnp.where(better, c, b) for c, b in zip(cand, best))
    _, e1, e2, a1, a2 = best
    eid_ref[0, 0:1, :] = e1
    eid_ref[0, 1:2, :] = e2
    wt_ref[0, 0:1, :] = a1 / (a1 + a2)
    wt_ref[0, 1:2, :] = a2 / (a1 + a2)

    e_iota = lax.broadcasted_iota(I32, (N_EXPERTS, tn), 0)
    oh1 = jnp.where(e_iota == e1, 1.0, 0.0)
    oh2 = jnp.where(e_iota == e2, 1.0, 0.0)
    tri = tri_s[...]
    tot1 = jnp.sum(oh1, axis=1, keepdims=True)
    tot2 = jnp.sum(oh2, axis=1, keepdims=True)
    base = carry_s[:, 0:1]
    c1 = base + jnp.dot(oh1.astype(tri.dtype), tri, preferred_element_type=F32)
    c2 = base + tot1 + jnp.dot(oh2.astype(tri.dtype), tri, preferred_element_type=F32)
    pos_ref[0, 0:1, :] = jnp.sum(oh1 * c1, axis=0, keepdims=True).astype(I32)
    pos_ref[0, 1:2, :] = jnp.sum(oh2 * c2, axis=0, keepdims=True).astype(I32)
    carry_s[...] = carry_s[...] + (tot1 + tot2)
    cnt_ref[...] = carry_s[...]


def _route_call(aff_t, b_router, tn):
    m = aff_t.shape[1]
    nt = m // tn
    tok = lambda dt: jax.ShapeDtypeStruct((nt, TOP_K, tn), dt)
    tok_spec = pl.BlockSpec((1, TOP_K, tn), lambda i: (i, 0, 0))
    return pl.pallas_call(
        _route_kernel,
        out_shape=(tok(I32), tok(F32), tok(I32), jax.ShapeDtypeStruct((N_EXPERTS, LANES), F32)),
        grid=(nt,),
        in_specs=[pl.BlockSpec((N_EXPERTS, tn), lambda i: (0, i)),
                  pl.BlockSpec((N_EXPERTS, 1), lambda i: (0, 0))],
        out_specs=(tok_spec, tok_spec, tok_spec, pl.BlockSpec((N_EXPERTS, LANES), lambda i: (0, 0))),
        scratch_shapes=[pltpu.VMEM((N_EXPERTS, LANES), F32), pltpu.VMEM((tn, tn), BF16)],
        compiler_params=_cparams("arbitrary"),
        name="route",
    )(aff_t, b_router.reshape(N_EXPERTS, 1))


def _dispatch_kernel(pstart_ref, eid_ref, pos_ref, h_ref, xs_in_ref, xs_ref, sem):
    del xs_in_ref
    tm = h_ref.shape[0]

    def row_copy(t, dst_row):
        return pltpu.make_async_copy(h_ref.at[pl.ds(t, 1)], xs_ref.at[pl.ds(dst_row, 1)], sem)

    def issue(t, c):
        for s in range(TOP_K):
            row_copy(t, pstart_ref[eid_ref[0, s, t]] + pos_ref[0, s, t]).start()
        return c

    def drain(t, c):
        for s in range(TOP_K):
            row_copy(0, 0).wait()
        return c

    lax.fori_loop(0, tm, issue, 0)
    lax.fori_loop(0, tm, drain, 0)


def _dispatch_call(pstart, eid, pos, h, n_rows):
    m, d = h.shape
    nt, _, tm = eid.shape
    smem_tok = pl.BlockSpec((1, TOP_K, tm), lambda i, ps: (i, 0, 0), memory_space=pltpu.SMEM)
    return pl.pallas_call(
        _dispatch_kernel,
        out_shape=jax.ShapeDtypeStruct((n_rows, d), F32),
        grid_spec=pltpu.PrefetchScalarGridSpec(
            num_scalar_prefetch=1,
            grid=(nt,),
            in_specs=[smem_tok, smem_tok, pl.BlockSpec((tm, d), lambda i, ps: (i, 0)),
                      pl.BlockSpec(memory_space=pl.ANY)],
            out_specs=pl.BlockSpec(memory_space=pl.ANY),
            scratch_shapes=[pltpu.SemaphoreType.DMA(())]),
        input_output_aliases={4: 0},
        compiler_params=_cparams("arbitrary"),
        name="dispatch",
    )(pstart, eid, pos, h, jnp.zeros((n_rows, d), F32))


def _expert_kernel(be_ref, nused_ref, xs_ref, wg_ref, wu_ref, wd_ref, ys_ref):
    j = pl.program_id(0)

    @pl.when(j < nused_ref[0])
    def _():
        x = xs_ref[...].astype(BF16)
        hid = _silu(_dot(x, wg_ref[0, 0])) * _dot(x, wu_ref[0, 0])
        ys_ref[...] = _dot(hid, wd_ref[0, 0])

    @pl.when(j >= nused_ref[0])
    def _():
        ys_ref[...] = jnp.zeros(ys_ref.shape, F32)


def _expert_call(block_e, n_used, xs, w_gate, w_up, w_down, layer):
    n_rows, d = xs.shape
    f = w_gate.shape[3]
    nb = n_rows // EXPERT_ROWS
    return pl.pallas_call(
        _expert_kernel,
        out_shape=jax.ShapeDtypeStruct((n_rows, d), F32),
        grid_spec=pltpu.PrefetchScalarGridSpec(
            num_scalar_prefetch=2,
            grid=(nb,),
            in_specs=[pl.BlockSpec((EXPERT_ROWS, d), lambda j, be, nu: (j, 0)),
                      pl.BlockSpec((1, 1, d, f), lambda j, be, nu: (layer, be[j], 0, 0)),
                      pl.BlockSpec((1, 1, d, f), lambda j, be, nu: (layer, be[j], 0, 0)),
                      pl.BlockSpec((1, 1, f, d), lambda j, be, nu: (layer, be[j], 0, 0))],
            out_specs=pl.BlockSpec((EXPERT_ROWS, d), lambda j, be, nu: (j, 0))),
        compiler_params=_cparams("arbitrary"),
        name="experts",
    )(block_e, n_used, xs, w_gate, w_up, w_down)


def _combine_kernel(pstart_ref, eid_ref, pos_ref, wt_ref, h_ref, ple_ref, ys_ref, g_ref, b_ref, o_ref,
                    buf, sem):
    tm = h_ref.shape[0]

    def row_copy(src_row, s, t):
        return pltpu.make_async_copy(ys_ref.at[pl.ds(src_row, 1)], buf.at[s, pl.ds(t, 1)], sem)

    def issue(t, c):
        for s in range(TOP_K):
            row_copy(pstart_ref[eid_ref[0, s, t]] + pos_ref[0, s, t], s, t).start()
        return c

    def drain(t, c):
        for s in range(TOP_K):
            row_copy(0, s, 0).wait()
        return c

    lax.fori_loop(0, tm, issue, 0)
    lax.fori_loop(0, tm, drain, 0)

    w_cols = jnp.concatenate([wt_ref[0], jnp.zeros((LANES - TOP_K, tm), F32)], axis=0).T
    ffn = buf[0] * w_cols[:, 0:1] + buf[1] * w_cols[:, 1:2]
    o_ref[...] = _layer_norm(DN_ALPHA * h_ref[...] + ffn + ple_ref[...], g_ref[...], b_ref[...])


def _combine_call(pstart, eid, pos, wt, h, ple, ys, ln_g, ln_b):
    m, d = h.shape
    nt, _, tm = eid.shape
    smem_tok = pl.BlockSpec((1, TOP_K, tm), lambda i, ps: (i, 0, 0), memory_space=pltpu.SMEM)
    rows = pl.BlockSpec((tm, d), lambda i, ps: (i, 0))
    vec = pl.BlockSpec((1, d), lambda i, ps: (0, 0))
    return pl.pallas_call(
        _combine_kernel,
        out_shape=jax.ShapeDtypeStruct((m, d), F32),
        grid_spec=pltpu.PrefetchScalarGridSpec(
            num_scalar_prefetch=1,
            grid=(nt,),
            in_specs=[smem_tok, smem_tok, pl.BlockSpec((1, TOP_K, tm), lambda i, ps: (i, 0, 0)),
                      rows, rows, pl.BlockSpec(memory_space=pl.ANY), vec, vec],
            out_specs=rows,
            scratch_shapes=[pltpu.VMEM((TOP_K, tm, d), F32), pltpu.SemaphoreType.DMA(())]),
        compiler_params=_cparams("arbitrary"),
        name="combine",
    )(pstart, eid, pos, wt, h, ple, ys, ln_g.reshape(1, d), ln_b.reshape(1, d))


SEQS_PER_STEP = SUBLANES


def _gdn_sample_kernel(new_ref, prev_ref, z_ref, ab_ref, w_ref, alog_ref, dtb_ref, ng_ref, s_ref,
                       y_ref, so_ref):
    w = w_ref[...]
    pre = prev_ref[0] * w[0:1]
    for j in range(1, GDN_CONV - 1):
        pre = pre + prev_ref[j] * w[j:j + 1]
    qkv = _silu(pre + new_ref[...] * w[GDN_CONV - 1:GDN_CONV])
    g_all, beta_all = _gdn_gates(ab_ref[...], alog_ref[...], dtb_ref[...])
    ng = ng_ref[...]
    z = z_ref[...]
    pad = jnp.zeros((LANES - SEQS_PER_STEP, GDN_DK), F32)
    for h in range(GDN_HEADS):
        q = _l2_normalize(qkv[:, h * GDN_DK:(h + 1) * GDN_DK]) * (GDN_DK ** -0.5)
        k = _l2_normalize(qkv[:, GDN_QK + h * GDN_DK:GDN_QK + (h + 1) * GDN_DK])
        v = qkv[:, 2 * GDN_QK + h * GDN_DV:2 * GDN_QK + (h + 1) * GDN_DV]
        eg = jnp.exp(g_all[:, h:h + 1])
        beta = beta_all[:, GDN_HEADS + h:GDN_HEADS + h + 1]
        qk = jnp.sum(q * k, axis=-1, keepdims=True)
        k_t = jnp.concatenate([k, pad], axis=0).T
        qg_t = jnp.concatenate([q * eg, pad], axis=0).T
        outs = []
        for i in range(SEQS_PER_STEP):
            s_mat = s_ref[0, i, h]
            k_col = k_t[:, i:i + 1]
            k_s = jnp.sum(k_col * s_mat, axis=0, keepdims=True)
            q_s = jnp.sum(qg_t[:, i:i + 1] * s_mat, axis=0, keepdims=True)
            v_new = beta[i:i + 1] * (v[i:i + 1] - eg[i:i + 1] * k_s)
            outs.append(q_s + qk[i:i + 1] * v_new)
            so_ref[i, h] = s_mat * eg[i:i + 1] + k_col * v_new
        o = jnp.concatenate(outs, axis=0)
        o = o * lax.rsqrt(jnp.mean(o * o, axis=-1, keepdims=True) + NORM_EPS) * ng
        y_ref[:, h * GDN_DV:(h + 1) * GDN_DV] = (o * _silu(z[:, h * GDN_DV:(h + 1) * GDN_DV])).astype(y_ref.dtype)


def _gdn_sample_call(conv_new, conv_prev_t, z, ab, conv_w, a_log, dt_bias, norm_g, state, layer):
    n, c = conv_new.shape
    sp = SEQS_PER_STEP
    alog_row = jnp.pad(a_log, (0, LANES - GDN_HEADS)).reshape(1, LANES)
    dtb_row = jnp.pad(dt_bias, (0, LANES - GDN_HEADS)).reshape(1, LANES)
    row_spec = pl.BlockSpec((1, LANES), lambda i: (0, 0))
    s_spec = pl.BlockSpec((sp, GDN_HEADS, GDN_DK, GDN_DV), lambda i: (i, 0, 0, 0))
    s_in_spec = pl.BlockSpec((1, sp, GDN_HEADS, GDN_DK, GDN_DV), lambda i: (layer, i, 0, 0, 0))
    return pl.pallas_call(
        _gdn_sample_kernel,
        out_shape=(jax.ShapeDtypeStruct((n, GDN_VW), BF16), jax.ShapeDtypeStruct(state.shape[1:], F32)),
        grid=(n // sp,),
        in_specs=[pl.BlockSpec((sp, c), lambda i: (i, 0)),
                  pl.BlockSpec((GDN_CONV - 1, sp, c), lambda i: (0, i, 0)),
                  pl.BlockSpec((sp, GDN_VW), lambda i: (i, 0)),
                  pl.BlockSpec((sp, LANES), lambda i: (i, 0)),
                  pl.BlockSpec((GDN_CONV, c), lambda i: (0, 0)),
                  row_spec, row_spec, row_spec, s_in_spec],
        out_specs=(pl.BlockSpec((sp, GDN_VW), lambda i: (i, 0)), s_spec),
        compiler_params=_cparams("parallel"),
        name="gdn_sample",
    )(conv_new, conv_prev_t, z, ab, conv_w, alog_row, dtb_row, norm_g.reshape(1, GDN_DV), state)


PAGES_PER_SEQ = PAST_LEN // PAGE_SIZE
PAGES_PER_BLOCK = MOBA_BLOCK // PAGE_SIZE
PAST_BLOCKS = PAST_LEN // MOBA_BLOCK


def _moba_sample_kernel(pt_ref, q_ref, kn_ref, vn_ref, rb_ref, *rest):
    del pt_ref
    np_ = PAGES_PER_SEQ
    k_pages = rest[:np_]
    v_pages = rest[np_:2 * np_]
    o_ref, bias_s = rest[2 * np_:]
    n = pl.program_id(0)
    nh, hd = MOBA_HEADS, MOBA_HD
    rb = rb_ref[...]

    def bias_col(bucket):
        return rb[:, bucket:bucket + 1]

    @pl.when(n == 0)
    def _():
        thr = np.asarray(T5_THRESHOLDS)
        for tt in range(PAGE_SIZE):
            bucket = int(np.sum(thr <= PAGE_SIZE - tt)) - 1
            bias_s[tt] = jnp.broadcast_to(bias_col(bucket), (nh, LANES))

    q = q_ref[0]
    qs = q * (hd ** -0.5)
    ones = jnp.ones((hd, LANES), BF16)

    def head_scores(kq):
        r = kq.shape[0]
        s = jnp.dot(kq.reshape(r * nh, hd).astype(BF16), ones, preferred_element_type=F32)
        return s.reshape(r, nh, LANES)

    gates = []
    for b in range(PAST_BLOCKS):
        tot = jnp.sum(k_pages[b * PAGES_PER_BLOCK][0, 0], axis=0)
        for j in range(1, PAGES_PER_BLOCK):
            tot = tot + jnp.sum(k_pages[b * PAGES_PER_BLOCK + j][0, 0], axis=0)
        gates.append(jnp.sum(tot * (1.0 / MOBA_BLOCK) * q, axis=-1, keepdims=True))
    picked = []
    for b in range(PAST_BLOCKS):
        rank = jnp.zeros((nh, 1), F32)
        for b2 in range(PAST_BLOCKS):
            if b2 != b:
                ahead = (gates[b2] > gates[b]) | (gates[b2] == gates[b]) if b2 < b else gates[b2] > gates[b]
                rank = rank + jnp.where(ahead, 1.0, 0.0)
        picked.append(rank < MOBA_TOPK)

    far = bias_col(REL_BUCKETS - 1)
    m_run = jnp.broadcast_to(jnp.sum(kn_ref[0] * qs, axis=-1, keepdims=True) + bias_col(0), (nh, LANES))
    l_run = jnp.ones((nh, LANES), F32)
    acc = vn_ref[0]
    for j in range(np_):
        s = head_scores(k_pages[j][0, 0] * qs)
        s = s + (bias_s[...] if j == np_ - 1 else far)
        s = jnp.where(picked[j // PAGES_PER_BLOCK], s, NEG_BIG)
        m_new = jnp.maximum(m_run, jnp.max(s, axis=0))
        alpha = jnp.exp(m_run - m_new)
        p = jnp.exp(s - m_new)
        l_run = alpha * l_run + jnp.sum(p, axis=0)
        acc = alpha[:, :hd] * acc + jnp.sum(p[:, :, :hd] * v_pages[j][0, 0], axis=0)
        m_run = m_new
    o_ref[0] = (acc / l_run[:, :hd]).astype(o_ref.dtype)


def _moba_sample_call(q, k_new, v_new, cache_k, cache_v, layer, page_table, rel_bias):
    n, nh, hd = q.shape
    np_ = PAGES_PER_SEQ
    rb = jnp.pad(rel_bias.T, ((0, 0), (0, LANES - REL_BUCKETS)))
    tok = pl.BlockSpec((1, nh, hd), lambda i, pt: (i, 0, 0))
    page = lambda j: pl.BlockSpec((1, 1, PAGE_SIZE, nh, hd),
                                  lambda i, pt: (layer, pt[i * np_ + j], 0, 0, 0))
    return pl.pallas_call(
        _moba_sample_kernel,
        out_shape=jax.ShapeDtypeStruct((n, nh, hd), BF16),
        grid_spec=pltpu.PrefetchScalarGridSpec(
            num_scalar_prefetch=1,
            grid=(n,),
            in_specs=[tok, tok, tok, pl.BlockSpec((nh, LANES), lambda i, pt: (0, 0))]
            + [page(j) for j in range(np_)] + [page(j) for j in range(np_)],
            out_specs=tok,
            scratch_shapes=[pltpu.VMEM((PAGE_SIZE, nh, LANES), F32)]),
        compiler_params=_cparams("arbitrary"),
        name="moba_sample",
    )(page_table.reshape(-1), q, k_new, v_new, rb, *([cache_k] * np_), *([cache_v] * np_))


def _moe_layout(counts_f32, n_rows):
    counts = counts_f32.astype(I32)
    padded = (counts + EXPERT_ROWS - 1) // EXPERT_ROWS * EXPERT_ROWS
    ends = jnp.cumsum(padded)
    pstart = ends - padded
    nb = n_rows // EXPERT_ROWS
    block_e = jnp.minimum(
        jnp.searchsorted(ends, jnp.arange(nb, dtype=I32) * EXPERT_ROWS, side='right'), N_EXPERTS - 1)
    n_used = (ends[-1:] // EXPERT_ROWS).astype(I32)
    return pstart.astype(I32), block_e.astype(I32), n_used


def _moe_rows(m):
    n_asg = m * TOP_K
    return -(-(n_asg + N_EXPERTS * (EXPERT_ROWS - 1)) // EXPERT_ROWS) * EXPERT_ROWS


def _token_tail(ya, yb, gate, x, p, layer, lw, tm):
    h, ple, aff_t = _merge_call(ya, yb, gate, x, p, layer, lw['wug'], lw['wum'], lw['wo'], lw['wpg'],
                                lw['wpp'], lw['wrt'], lw['ln1_g'], lw['ln1_b'], tm)
    eid, wt, pos, cnt = _route_call(aff_t, lw['b_router'], tm)
    n_rows = _moe_rows(x.shape[0])
    pstart, block_e, n_used = _moe_layout(cnt[:, 0], n_rows)
    xs = _dispatch_call(pstart, eid, pos, h, n_rows)
    ys = _expert_call(block_e, n_used, xs, lw['w_exp_gate'], lw['w_exp_up'], lw['w_exp_down'], layer)
    return _combine_call(pstart, eid, pos, wt, h, ple, ys, lw['ln2_g'], lw['ln2_b'])


def kernel(x_prompt, x_sample, cache_k, cache_v, state_gdn, state_conv, page_table, p_prompt, p_sample,
           ln0_g, ln0_b, w_in, gdn_conv_w, gdn_a_log, gdn_dt_bias, gdn_norm_g, w_up_gdn, w_up_moba, w_o,
           ln1_g, ln1_b, rel_bias, w_router, b_router, w_exp_gate, w_exp_up, w_exp_down,
           w_ple_gate, w_ple_proj, ln2_g, ln2_b):
    assert GDN_DK == LANES and GDN_DV == LANES
    n_p, t_p, d = x_prompt.shape
    n_s, t_s, _ = x_sample.shape
    assert t_s == 1 and t_p % MOBA_BLOCK == 0 and n_s % SEQS_PER_STEP == 0
    m_p, m_s = n_p * t_p, n_s
    tm_p, tm_s = _row_tile(m_p, 256), _row_tile(m_s, 256)
    pp = p_prompt.reshape(DEPTH, m_p, PLE_DIM)
    ps = p_sample.reshape(DEPTH, m_s, PLE_DIM)
    heads = lambda a: a.reshape(a.shape[0], MOBA_HEADS, MOBA_HD)

    xp = _ln_call(x_prompt.reshape(m_p, d), ln0_g, ln0_b)
    xs = _ln_call(x_sample.reshape(m_s, d), ln0_g, ln0_b)
    wrt = w_router.T
    outs = {name: [] for name in ('kp', 'vp', 'sp', 'cp', 'ks', 'vs', 'ss', 'cs')}
    for i in range(DEPTH):
        w_parts = _split_w_in(w_in[i])
        lw = dict(wug=w_up_gdn[i].astype(BF16), wum=w_up_moba[i].astype(BF16), wo=w_o[i].astype(BF16),
                  wpg=w_ple_gate[i].astype(BF16), wpp=w_ple_proj[i].astype(BF16), wrt=wrt,
                  ln1_g=ln1_g[i], ln1_b=ln1_b[i], ln2_g=ln2_g[i], ln2_b=ln2_b[i], b_router=b_router,
                  w_exp_gate=w_exp_gate, w_exp_up=w_exp_up, w_exp_down=w_exp_down)

        conv, z, ab, q, k, v, gate = _proj_call(xp, w_parts)
        seq = lambda a: a.reshape(n_p, t_p, a.shape[-1])
        ya, s_new = _gdn_prompt_call(seq(conv), seq(z), seq(ab), gdn_conv_w[i], gdn_a_log[i],
                                     gdn_dt_bias[i], gdn_norm_g[i])
        yb = _moba_prompt_call(seq(q), seq(k), seq(v), rel_bias)
        xp = _token_tail(ya.reshape(m_p, GDN_VW), yb.reshape(m_p, MOBA_W), gate, xp, pp, i, lw, tm_p)
        outs['kp'].append(k.reshape(n_p, t_p, MOBA_HEADS, MOBA_HD))
        outs['vp'].append(v.reshape(n_p, t_p, MOBA_HEADS, MOBA_HD))
        outs['sp'].append(s_new)
        outs['cp'].append(seq(conv)[:, t_p - (GDN_CONV - 1):, :])

        conv, z, ab, q, k, v, gate = _proj_call(xs, w_parts)
        ya, s_new = _gdn_sample_call(conv, state_conv[i].transpose(1, 0, 2), z, ab, gdn_conv_w[i],
                                     gdn_a_log[i], gdn_dt_bias[i], gdn_norm_g[i], state_gdn, i)
        yb = _moba_sample_call(heads(q), heads(k), heads(v), cache_k, cache_v, i, page_table, rel_bias)
        xs = _token_tail(ya, yb.reshape(m_s, MOBA_W), gate, xs, ps, i, lw, tm_s)
        outs['ks'].append(k.reshape(n_s, 1, MOBA_HEADS, MOBA_HD))
        outs['vs'].append(v.reshape(n_s, 1, MOBA_HEADS, MOBA_HD))
        outs['ss'].append(s_new)
        outs['cs'].append(jnp.concatenate([state_conv[i][:, 1:, :], conv[:, None, :]], axis=1))

    st = lambda name: jnp.stack(outs[name])
    return (xp.reshape(n_p, t_p, d), xs.reshape(n_s, 1, d), st('kp'), st('vp'), st('sp'), st('cp'),
            st('ks'), st('vs'), st('ss'), st('cs'))
```

```python
import functools
import math

import numpy as np
import jax
import jax.numpy as jnp
from jax import lax
from jax.experimental import pallas as pl
from jax.experimental.pallas import tpu as pltpu

F32 = jnp.float32
BF16 = jnp.bfloat16
I32 = jnp.int32
HIGHEST = lax.Precision.HIGHEST

D_MODEL = 1024
DEPTH = 4
PAST_LEN = 2048
PAGE_SIZE = 128
GDN_HEADS = 4
GDN_DK = 128
GDN_DV = 128
GDN_CONV = 4
GDN_CHUNK = 64
MOBA_HEADS = 8
MOBA_HD = 64
MOBA_BLOCK = 256
MOBA_TOPK = 3
REL_BUCKETS = 32
REL_MAX_DIST = 128
N_EXPERTS = 32
N_GROUPS = 4
EXPERTS_PER_GROUP = N_EXPERTS // N_GROUPS
TOP_K = 2
D_FF_EXPERT = 512
PLE_DIM = 256
DN_ALPHA = (2 * DEPTH) ** 0.25
LN_EPS = 1e-5
NORM_EPS = 1e-6

GDN_QK = GDN_HEADS * GDN_DK
GDN_VW = GDN_HEADS * GDN_DV
GDN_CONV_CH = 2 * GDN_QK + GDN_VW
MOBA_W = MOBA_HEADS * MOBA_HD
COL_Z = GDN_CONV_CH
COL_A = COL_Z + GDN_VW
COL_B = COL_A + GDN_HEADS
COL_MOBA = COL_B + GDN_HEADS
COL_GATE = COL_MOBA + 3 * MOBA_W
N_IN = COL_GATE + 2 * D_MODEL

LANES = 128
SUBLANES = 8
VMEM_LIMIT_BYTES = 56 * 1024 * 1024

EXPERT_ROWS = 256
GDN_GROUP = 256
NEG_BIG = -1e30


def _cparams(*sem):
    return pltpu.CompilerParams(dimension_semantics=sem, vmem_limit_bytes=VMEM_LIMIT_BYTES)


def _row_tile(m, target):
    t = min(m, target)
    assert m % t == 0, (m, t)
    return t


def _t5_thresholds():
    exact = REL_BUCKETS // 2
    n = np.arange(0, 2 * REL_MAX_DIST)
    log_ratio = np.log(np.maximum(n, 1).astype(np.float32) / np.float32(exact)) / np.float32(
        math.log(REL_MAX_DIST / exact))
    large = exact + (log_ratio * np.float32(REL_BUCKETS - exact)).astype(np.int32)
    bucket = np.where(n < exact, n, np.minimum(large, REL_BUCKETS - 1))
    assert np.all(np.diff(bucket) >= 0)
    return [int(np.argmax(bucket >= k)) for k in range(REL_BUCKETS)]


T5_THRESHOLDS = _t5_thresholds()


def _layer_norm(x, g, b):
    mu = jnp.mean(x, axis=-1, keepdims=True)
    xc = x - mu
    var = jnp.mean(xc * xc, axis=-1, keepdims=True)
    return xc * lax.rsqrt(var + LN_EPS) * g + b


def _sigmoid(x):
    return 1.0 / (1.0 + jnp.exp(-x))


def _silu(x):
    return x * _sigmoid(x)


def _softplus(x):
    return jnp.maximum(x, 0.0) + jnp.log(1.0 + jnp.exp(-jnp.abs(x)))


def _dot(a, b):
    return jnp.dot(a.astype(BF16), b.astype(BF16), preferred_element_type=F32)


def _dot_nt(a, b, precision=None):
    if precision is None:
        a, b = a.astype(BF16), b.astype(BF16)
    return lax.dot_general(a, b, (((1,), (1,)), ((), ())), precision=precision,
                           preferred_element_type=F32)


def _dot_tn(a, b, precision=None):
    if precision is None:
        a, b = a.astype(BF16), b.astype(BF16)
    return lax.dot_general(a, b, (((0,), (0,)), ((), ())), precision=precision,
                           preferred_element_type=F32)


def _ln_kernel(x_ref, g_ref, b_ref, o_ref):
    o_ref[...] = _layer_norm(x_ref[...], g_ref[...], b_ref[...])


def _ln_call(x, g, b):
    m, d = x.shape
    tm = _row_tile(m, 512)
    return pl.pallas_call(
        _ln_kernel,
        out_shape=jax.ShapeDtypeStruct((m, d), F32),
        grid=(m // tm,),
        in_specs=[pl.BlockSpec((tm, d), lambda i: (i, 0)),
                  pl.BlockSpec((1, d), lambda i: (0, 0)),
                  pl.BlockSpec((1, d), lambda i: (0, 0))],
        out_specs=pl.BlockSpec((tm, d), lambda i: (i, 0)),
        compiler_params=_cparams("parallel"),
        name="ln0",
    )(x, g.reshape(1, d), b.reshape(1, d))


N_ROW_PARTS = 5


def _proj_kernel(x_ref, wc_ref, wz_ref, wab_ref, wq_ref, wg_ref, wkt_ref, wvt_ref,
                 conv_ref, z_ref, ab_ref, q_ref, gate_ref, kt_ref, vt_ref):
    x = x_ref[...].astype(BF16)
    for w_ref, o_ref in ((wc_ref, conv_ref), (wz_ref, z_ref), (wab_ref, ab_ref), (wq_ref, q_ref),
                         (wg_ref, gate_ref)):
        o_ref[...] = jnp.dot(x, w_ref[...], preferred_element_type=F32)
    kt_ref[0] = _dot_nt(wkt_ref[...], x)
    vt_ref[0] = _dot_nt(wvt_ref[...], x)


def _split_w_in(w):
    wab = jnp.pad(w[:, COL_A:COL_MOBA], ((0, 0), (0, LANES - 2 * GDN_HEADS)))
    parts = (w[:, :COL_Z], w[:, COL_Z:COL_A], wab, w[:, COL_MOBA:COL_MOBA + MOBA_W], w[:, COL_GATE:],
             w[:, COL_MOBA + MOBA_W:COL_MOBA + 2 * MOBA_W].T, w[:, COL_MOBA + 2 * MOBA_W:COL_GATE].T)
    return tuple(p.astype(BF16) for p in parts)


def _proj_call(x, w_parts, n_seq):
    m, d = x.shape
    t_len = m // n_seq
    tm = _row_tile(t_len, 256)
    tiles = t_len // tm
    row_w = [p.shape[1] for p in w_parts[:N_ROW_PARTS]]
    col_w = [p.shape[0] for p in w_parts[N_ROW_PARTS:]]
    return pl.pallas_call(
        _proj_kernel,
        out_shape=tuple(jax.ShapeDtypeStruct((m, n), F32) for n in row_w)
        + tuple(jax.ShapeDtypeStruct((n_seq, n, t_len), F32) for n in col_w),
        grid=(m // tm,),
        in_specs=[pl.BlockSpec((tm, d), lambda i: (i, 0))]
        + [pl.BlockSpec(p.shape, lambda i: (0, 0)) for p in w_parts],
        out_specs=tuple(pl.BlockSpec((tm, n), lambda i: (i, 0)) for n in row_w)
        + tuple(pl.BlockSpec((1, n, tm), lambda i: (i // tiles, 0, i % tiles)) for n in col_w),
        compiler_params=_cparams("parallel"),
        name="proj",
    )(x, *w_parts)


def _chunk_cumsum(x, pos_in_chunk, axis):
    s = 1
    while s < GDN_CHUNK:
        x = x + jnp.where(pos_in_chunk >= s, pltpu.roll(x, s, axis), 0.0)
        s *= 2
    return x


def _l2_normalize(x):
    return x * lax.rsqrt(jnp.sum(x * x, axis=-1, keepdims=True) + NORM_EPS)


def _gdn_gates(ab, alog_row, dtb_row):
    g = -jnp.exp(alog_row) * _softplus(ab + dtb_row)
    return g, _sigmoid(ab)


GDN_GROUPS_PER_ITER = 4


def _gdn_prompt_kernel(conv_ref, z_ref, ab_ref, w_ref, alog_ref, dtb_ref, ng_ref, y_ref, s_ref,
                       qn_s, kn_s, kb_s, vb_s, gcb_s, gcr_s, u_s, w_s, qg_s, kg_s, intra_s, egl_s):
    t_len = conv_ref.shape[1]
    n_groups = t_len // GDN_GROUP
    n_chunks = t_len // GDN_CHUNK
    cpg = GDN_GROUP // GDN_CHUNK
    groups_per_iter = math.gcd(n_groups, GDN_GROUPS_PER_ITER)

    row = lax.broadcasted_iota(I32, (t_len, LANES), 0)
    lane = lax.broadcasted_iota(I32, (t_len, LANES), 1)

    ab_t = ab_ref[0].T[0:SUBLANES]
    g_rows = -jnp.exp(alog_ref[:, 0:1]) * _softplus(ab_t + dtb_ref[:, 0:1])
    lane_t = lax.broadcasted_iota(I32, (SUBLANES, t_len), 1)
    sub_t = lax.broadcasted_iota(I32, (SUBLANES, t_len), 0)
    gc_rows = _chunk_cumsum(g_rows, lane_t % GDN_CHUNK, 1)
    packed = jnp.where(sub_t < GDN_HEADS, gc_rows, _sigmoid(ab_t))
    cols_all = jnp.concatenate([packed, jnp.zeros((LANES - SUBLANES, t_len), F32)], axis=0).T

    ii = lax.broadcasted_iota(I32, (GDN_GROUP, GDN_GROUP), 0)
    jj = lax.broadcasted_iota(I32, (GDN_GROUP, GDN_GROUP), 1)
    same = (ii // GDN_CHUNK) == (jj // GDN_CHUNK)
    incl = same & (ii >= jj)
    strict = same & (ii > jj)
    eye = (ii == jj).astype(F32)

    def conv(col0, h):
        cols = pl.ds(pl.multiple_of(col0 + h * LANES, LANES), LANES)
        x = conv_ref[0, :, cols]
        w = w_ref[:, cols]
        y = x * w[GDN_CONV - 1:GDN_CONV]
        for j in range(1, GDN_CONV):
            y = y + jnp.where(row >= j, pltpu.roll(x, j, 0), 0.0) * w[GDN_CONV - 1 - j:GDN_CONV - j]
        return _silu(y)

    def head_body(h, carry):
        qn = _l2_normalize(conv(0, h)) * (GDN_DK ** -0.5)
        kn = _l2_normalize(conv(GDN_QK, h))
        vv = conv(2 * GDN_QK, h)
        gc_col = jnp.sum(jnp.where(lane == h, cols_all, 0.0), axis=1, keepdims=True)
        beta_col = jnp.sum(jnp.where(lane == h + GDN_HEADS, cols_all, 0.0), axis=1, keepdims=True)
        gc_row = jnp.sum(jnp.where(sub_t == h, gc_rows, 0.0), axis=0, keepdims=True)
        qn_s[...] = qn
        kn_s[...] = kn
        kb_s[...] = kn * beta_col
        vb_s[...] = vv * beta_col
        gcb_s[...] = jnp.broadcast_to(gc_col, (t_len, LANES))
        for g in range(n_groups):
            gcr_s[g] = jnp.broadcast_to(gc_row[:, g * GDN_GROUP:(g + 1) * GDN_GROUP], (SUBLANES, GDN_GROUP))

        def group_iter(it, c2):
            gs = [it * groups_per_iter + gg for gg in range(groups_per_iter)]
            r0s = [pl.multiple_of(g * GDN_GROUP, GDN_GROUP) for g in gs]
            rows = [pl.ds(r0, GDN_GROUP) for r0 in r0s]
            q_g = [qn_s[r, :] for r in rows]
            k_g = [kn_s[r, :] for r in rows]
            kb_g = [kb_s[r, :] for r in rows]
            vb_g = [vb_s[r, :] for r in rows]
            gcb_g = [gcb_s[r, :] for r in rows]
            gcol = [x[:, 0:1] for x in gcb_g]
            eg = [jnp.exp(x) for x in gcol]
            decay = [jnp.where(incl, jnp.exp(jnp.where(incl, gc - gcr_s[g][0:1, :], 0.0)), 0.0)
                     for gc, g in zip(gcol, gs)]
            kk = [_dot_nt(a, b) for a, b in zip(kb_g, k_g)]
            qk = [_dot_nt(a, b) for a, b in zip(q_g, k_g)]
            a_mat = [jnp.where(strict, x * d, 0.0) for x, d in zip(kk, decay)]
            intra = [x * d for x, d in zip(qk, decay)]
            t_inv = [eye - a for a in a_mat]
            x_pow = a_mat
            p = 2
            while p < GDN_CHUNK:
                x_pow = [_dot(x, x) for x in x_pow]
                t_inv = [t + _dot(t, x) for t, x in zip(t_inv, x_pow)]
                p *= 2
            uw = [_dot(t, jnp.concatenate([vb, kb * e], axis=1)) for t, vb, kb, e in zip(t_inv, vb_g, kb_g, eg)]
            for i, g in enumerate(gs):
                u_s[h, rows[i], :] = uw[i][:, :GDN_DV]
                w_s[h, rows[i], :] = uw[i][:, GDN_DV:].astype(w_s.dtype)
                qg_s[h, rows[i], :] = (q_g[i] * eg[i]).astype(qg_s.dtype)
                for c in range(cpg):
                    lo = c * GDN_CHUNK
                    g_last = gcb_g[i][lo + GDN_CHUNK - 1:lo + GDN_CHUNK, :]
                    rows_c = pl.ds(pl.multiple_of(r0s[i] + lo, GDN_CHUNK), GDN_CHUNK)
                    kg = k_g[i][lo:lo + GDN_CHUNK] * jnp.exp(g_last - gcb_g[i][lo:lo + GDN_CHUNK])
                    kg_s[h, rows_c, :] = kg.astype(kg_s.dtype)
                    intra_s[h, rows_c, :] = intra[i][lo:lo + GDN_CHUNK, lo:lo + GDN_CHUNK].astype(intra_s.dtype)
                    egl_s[h, g * cpg + c] = jnp.broadcast_to(jnp.exp(g_last), (SUBLANES, LANES))
            return c2

        lax.fori_loop(0, n_groups // groups_per_iter, group_iter, 0)
        return carry

    lax.fori_loop(0, GDN_HEADS, head_body, 0)

    ng = ng_ref[...]

    def chunk_body(c, s_mats):
        r0 = pl.multiple_of(c * GDN_CHUNK, GDN_CHUNK)
        rows = pl.ds(r0, GDN_CHUNK)
        heads = range(GDN_HEADS)
        wq = [jnp.concatenate([w_s[h, rows, :], qg_s[h, rows, :]], axis=0) for h in heads]
        ws_qs = [_dot(wq[h], s_mats[h]) for h in heads]
        v_new = [u_s[h, rows, :] - ws_qs[h][:GDN_CHUNK] for h in heads]
        s_out = [s_mats[h] * egl_s[h, c][0:1, :] + _dot_tn(kg_s[h, rows, :], v_new[h]) for h in heads]
        o_all = [ws_qs[h][GDN_CHUNK:] + _dot(intra_s[h, rows, :], v_new[h]) for h in heads]
        for h in heads:
            cols = slice(h * GDN_DV, (h + 1) * GDN_DV)
            o = o_all[h]
            o = o * lax.rsqrt(jnp.mean(o * o, axis=-1, keepdims=True) + NORM_EPS) * ng * _silu(z_ref[0, rows, cols])
            y_ref[0, rows, cols] = o.astype(y_ref.dtype)
        return tuple(s_out)

    zero = jnp.zeros((GDN_DK, GDN_DV), F32)
    s_fin = lax.fori_loop(0, n_chunks, chunk_body, (zero,) * GDN_HEADS)
    for h in range(GDN_HEADS):
        s_ref[0, h] = s_fin[h]


def _gdn_prompt_call(conv, z, ab, conv_w, a_log, dt_bias, norm_g):
    n, t_len, c = conv.shape
    tile = lambda v: jnp.broadcast_to(jnp.pad(v, (0, SUBLANES - GDN_HEADS))[:, None], (SUBLANES, LANES))
    once = pl.Buffered(1)
    small = lambda shape: pl.BlockSpec(shape, lambda i: (0, 0))
    per_head = lambda w, dt: pltpu.VMEM((GDN_HEADS, t_len, w), dt)
    big = pltpu.VMEM((t_len, LANES), F32)
    return pl.pallas_call(
        _gdn_prompt_kernel,
        out_shape=(jax.ShapeDtypeStruct((n, t_len, GDN_VW), BF16),
                   jax.ShapeDtypeStruct((n, GDN_HEADS, GDN_DK, GDN_DV), F32)),
        grid=(n,),
        in_specs=[pl.BlockSpec((1, t_len, c), lambda i: (i, 0, 0), pipeline_mode=once),
                  pl.BlockSpec((1, t_len, GDN_VW), lambda i: (i, 0, 0), pipeline_mode=once),
                  pl.BlockSpec((1, t_len, LANES), lambda i: (i, 0, 0)),
                  small((GDN_CONV, c)), small((SUBLANES, LANES)), small((SUBLANES, LANES)),
                  small((1, GDN_DV))],
        out_specs=(pl.BlockSpec((1, t_len, GDN_VW), lambda i: (i, 0, 0)),
                   pl.BlockSpec((1, GDN_HEADS, GDN_DK, GDN_DV), lambda i: (i, 0, 0, 0))),
        scratch_shapes=[big, big, big, big, big,
                        pltpu.VMEM((t_len // GDN_GROUP, SUBLANES, GDN_GROUP), F32),
                        per_head(GDN_DV, F32), per_head(GDN_DK, BF16), per_head(GDN_DK, BF16),
                        per_head(GDN_DK, BF16), per_head(GDN_CHUNK, BF16),
                        pltpu.VMEM((GDN_HEADS, t_len // GDN_CHUNK, SUBLANES, LANES), F32)],
        compiler_params=_cparams("parallel"),
        name="gdn_prompt",
    )(conv, z, ab, conv_w, tile(a_log), tile(dt_bias), norm_g.reshape(1, GDN_DV))


HEADS_PER_TILE = LANES // MOBA_HD


def _t5_bias(dist, rb_ref, head):
    b = jnp.full(dist.shape, rb_ref[head, 0], F32)
    for k in range(1, REL_BUCKETS):
        b = jnp.where(dist >= T5_THRESHOLDS[k], rb_ref[head, k], b)
    return b


def _top_blocks(gate_t, n_valid):
    nq = gate_t.shape[1]
    sub = lax.broadcasted_iota(I32, gate_t.shape, 0)
    g = jnp.where(sub < n_valid, gate_t, -jnp.inf)
    rank = jnp.zeros(g.shape, F32)
    for b2 in range(SUBLANES):
        row = g[b2:b2 + 1, :]
        rank = rank + jnp.where(row > g, 1.0, 0.0) + jnp.where(row == g, jnp.where(sub > b2, 1.0, 0.0), 0.0)
    sel_t = jnp.where((rank < MOBA_TOPK) & (sub < n_valid), 1.0, 0.0)
    return jnp.concatenate([sel_t, jnp.zeros((LANES - SUBLANES, nq), F32)], axis=0).T


def _moba_prompt_kernel(rb_ref, q_ref, kt_ref, vt_ref, o_ref, kmean_s, biasd_s, biasp_s, logit_s):
    hp = pl.program_id(0)
    n = pl.program_id(1)
    qb = pl.program_id(2)
    blk = MOBA_BLOCK
    n_blocks = kt_ref.shape[2] // blk
    half = blk // 2
    ii = lax.broadcasted_iota(I32, (blk, blk), 0)
    jj = lax.broadcasted_iota(I32, (blk, blk), 1)
    lane = lax.broadcasted_iota(I32, (blk, LANES), 1)

    @pl.when((n == 0) & (qb == 0))
    def _():
        for e in range(HEADS_PER_TILE):
            head = hp * HEADS_PER_TILE + e
            biasd_s[e] = _t5_bias(ii - jj, rb_ref, head)
            biasp_s[e] = _t5_bias(blk + ii - jj, rb_ref, head)

    @pl.when(qb == 0)
    def _():
        cols = [jnp.mean(kt_ref[0, :, b * blk:(b + 1) * blk], axis=1, keepdims=True) for b in range(n_blocks)]
        cols.append(jnp.zeros((LANES, LANES - n_blocks), F32))
        kmean_s[...] = jnp.concatenate(cols, axis=1).T

    q2 = q_ref[0]
    own_cols = pl.ds(pl.multiple_of(qb * blk, blk), blk)
    kt_own = kt_ref[0, :, own_cols].astype(BF16)
    scale = MOBA_HD ** -0.5

    q_scaled, sels, fars, m_init = [], [], [], []
    for e in range(HEADS_PER_TILE):
        qh = jnp.where(lane // MOBA_HD == e, q2, 0.0)
        gate_t = _dot_nt(kmean_s[0:SUBLANES, :], qh, precision=HIGHEST)
        sels.append(_top_blocks(gate_t, qb))
        fars.append(rb_ref[hp * HEADS_PER_TILE + e, REL_BUCKETS - 1])
        qs = (qh * scale).astype(BF16)
        q_scaled.append(qs)
        s = jnp.where(ii >= jj, jnp.dot(qs, kt_own, preferred_element_type=F32) + biasd_s[e], NEG_BIG)
        logit_s[e, :, own_cols] = s
        m_init.append(jnp.maximum(s[:, :half], s[:, half:]))

    def logits_of_past_block(b, m_parts):
        cols = pl.ds(pl.multiple_of(b * blk, blk), blk)
        kt_b = kt_ref[0, :, cols].astype(BF16)
        heads = range(HEADS_PER_TILE)
        raw = [jnp.dot(q_scaled[e], kt_b, preferred_element_type=F32) for e in heads]
        picked = [jnp.sum(jnp.where(lane == b, sels[e], 0.0), axis=-1, keepdims=True) for e in heads]
        s = [jnp.where(picked[e] > 0.0, raw[e] + jnp.where(b == qb - 1, biasp_s[e], fars[e]), NEG_BIG)
             for e in heads]
        for e in heads:
            logit_s[e, :, cols] = s[e]
        return tuple(jnp.maximum(m_parts[e], jnp.maximum(s[e][:, :half], s[e][:, half:])) for e in heads)

    m_parts = lax.fori_loop(0, qb, logits_of_past_block, tuple(m_init))
    m_rows = [jnp.max(mp, axis=-1, keepdims=True) for mp in m_parts]

    def weigh_block(b, carry):
        cols = pl.ds(pl.multiple_of(b * blk, blk), blk)
        vt_b = vt_ref[0, :, cols].astype(BF16)
        heads = range(HEADS_PER_TILE)
        p = [jnp.exp(logit_s[e, :, cols] - m_rows[e]) for e in heads]
        pv = [_dot_nt(p[e], vt_b) for e in heads]
        out = []
        for e in heads:
            out += [carry[2 * e] + p[e][:, :half] + p[e][:, half:], carry[2 * e + 1] + pv[e]]
        return tuple(out)

    zero = jnp.zeros((blk, LANES), F32)
    fin = lax.fori_loop(0, qb + 1, weigh_block, (zero,) * (2 * HEADS_PER_TILE))
    res = fin[1] / jnp.sum(fin[0], axis=-1, keepdims=True)
    for e in range(1, HEADS_PER_TILE):
        res = jnp.where(lane // MOBA_HD == e, fin[2 * e + 1] / jnp.sum(fin[2 * e], axis=-1, keepdims=True), res)
    o_ref[0] = res.astype(o_ref.dtype)


def _moba_prompt_call(q, kt, vt, rel_bias):
    n, t_len, _ = q.shape
    blk = MOBA_BLOCK
    assert blk == 2 * LANES and t_len // blk <= SUBLANES
    n_tiles = MOBA_W // LANES
    kv_spec = pl.BlockSpec((1, LANES, t_len), lambda hp, i, qb, rb: (i, hp, 0))
    q_spec = pl.BlockSpec((1, blk, LANES), lambda hp, i, qb, rb: (i, qb, hp))
    return pl.pallas_call(
        _moba_prompt_kernel,
        out_shape=jax.ShapeDtypeStruct((n, t_len, MOBA_W), BF16),
        grid_spec=pltpu.PrefetchScalarGridSpec(
            num_scalar_prefetch=1,
            grid=(n_tiles, n, t_len // blk),
            in_specs=[q_spec, kv_spec, kv_spec],
            out_specs=q_spec,
            scratch_shapes=[pltpu.VMEM((LANES, LANES), F32),
                            pltpu.VMEM((HEADS_PER_TILE, blk, blk), F32),
                            pltpu.VMEM((HEADS_PER_TILE, blk, blk), F32),
                            pltpu.VMEM((HEADS_PER_TILE, blk, t_len), F32)]),
        compiler_params=_cparams("arbitrary", "arbitrary", "arbitrary"),
        name="moba_prompt",
    )(rel_bias.T, q, kt, vt)


def _merge_kernel(ya_ref, yb_ref, gate_ref, x_ref, p_ref, wug_ref, wum_ref, wo_ref, wpg_ref, wpp_ref,
                  wrt_ref, g_ref, b_ref, h_ref, ple_ref, aff_ref):
    gate = _sigmoid(gate_ref[...])
    up_a = jnp.dot(ya_ref[...], wug_ref[...], preferred_element_type=F32)
    up_b = jnp.dot(yb_ref[...], wum_ref[...], preferred_element_type=F32)
    merged = gate[:, :D_MODEL] * up_a + gate[:, D_MODEL:] * up_b
    h = _layer_norm(DN_ALPHA * x_ref[...] + _dot(merged, wo_ref[...]), g_ref[...], b_ref[...])
    h_ref[...] = h
    ple_ref[...] = _sigmoid(_dot(h, wpg_ref[...])) * _dot(p_ref[0], wpp_ref[...])
    aff_ref[...] = _sigmoid(_dot_nt(wrt_ref[...], h, precision=HIGHEST))


def _merge_call(ya, yb, gate, x, p, layer, wug, wum, wo, wpg, wpp, wrt, ln_g, ln_b, tm):
    m, d = x.shape
    rows = lambda n: pl.BlockSpec((tm, n), lambda i: (i, 0))
    full = lambda a: pl.BlockSpec(a.shape, lambda i: (0, 0))
    return pl.pallas_call(
        _merge_kernel,
        out_shape=(jax.ShapeDtypeStruct((m, d), F32), jax.ShapeDtypeStruct((m, d), F32),
                   jax.ShapeDtypeStruct((N_EXPERTS, m), F32)),
        grid=(m // tm,),
        in_specs=[rows(GDN_VW), rows(MOBA_W), rows(2 * d), rows(d),
                  pl.BlockSpec((1, tm, PLE_DIM), lambda i: (layer, i, 0)),
                  full(wug), full(wum), full(wo), full(wpg), full(wpp), full(wrt),
                  pl.BlockSpec((1, d), lambda i: (0, 0)), pl.BlockSpec((1, d), lambda i: (0, 0))],
        out_specs=(rows(d), rows(d), pl.BlockSpec((N_EXPERTS, tm), lambda i: (0, i))),
        compiler_params=_cparams("parallel"),
        name="merge",
    )(ya, yb, gate, x, p, wug, wum, wo, wpg, wpp, wrt, ln_g.reshape(1, d), ln_b.reshape(1, d))


def _route_kernel(aff_ref, br_ref, eid_ref, wt_ref, pos_ref, cnt_ref, carry_s, tri_s):
    i = pl.program_id(0)
    tn = aff_ref.shape[1]

    @pl.when(i == 0)
    def _():
        carry_s[...] = jnp.zeros(carry_s.shape, F32)
        r = lax.broadcasted_iota(I32, (tn, tn), 0)
        c = lax.broadcasted_iota(I32, (tn, tn), 1)
        tri_s[...] = jnp.where(r < c, 1.0, 0.0).astype(tri_s.dtype)

    aff = aff_ref[...]
    sel = aff + br_ref[...]
    gsz = EXPERTS_PER_GROUP
    sub = lax.broadcasted_iota(I32, (gsz, tn), 0)
    best = None
    for g in range(N_GROUPS):
        v = sel[g * gsz:(g + 1) * gsz]
        a = aff[g * gsz:(g + 1) * gsz]
        m1 = jnp.max(v, axis=0, keepdims=True)
        i1 = jnp.min(jnp.where(v == m1, sub, gsz), axis=0, keepdims=True)
        v2 = jnp.where(sub == i1, -jnp.inf, v)
        m2 = jnp.max(v2, axis=0, keepdims=True)
        i2 = jnp.min(jnp.where(v2 == m2, sub, gsz), axis=0, keepdims=True)
        a1 = jnp.sum(jnp.where(sub == i1, a, 0.0), axis=0, keepdims=True)
        a2 = jnp.sum(jnp.where(sub == i2, a, 0.0), axis=0, keepdims=True)
        cand = (m1 + m2, i1 + g * gsz, i2 + g * gsz, a1, a2)
        if best is None:
            best = cand
        else:
            better = cand[0] > best[0]
            best = tuple(jnp.where(better, c, b) for c, b in zip(cand, best))
    _, e1, e2, a1, a2 = best
    eid_ref[0, 0:1, :] = e1
    eid_ref[0, 1:2, :] = e2
    wt_ref[0, 0:1, :] = a1 / (a1 + a2)
    wt_ref[0, 1:2, :] = a2 / (a1 + a2)

    e_iota = lax.broadcasted_iota(I32, (N_EXPERTS, tn), 0)
    oh1 = jnp.where(e_iota == e1, 1.0, 0.0)
    oh2 = jnp.where(e_iota == e2, 1.0, 0.0)
    tri = tri_s[...]
    tot1 = jnp.sum(oh1, axis=1, keepdims=True)
    tot2 = jnp.sum(oh2, axis=1, keepdims=True)
    base = carry_s[:, 0:1]
    c1 = base + jnp.dot(oh1.astype(tri.dtype), tri, preferred_element_type=F32)
    c2 = base + tot1 + jnp.dot(oh2.astype(tri.dtype), tri, preferred_element_type=F32)
    pos_ref[0, 0:1, :] = jnp.sum(oh1 * c1, axis=0, keepdims=True).astype(I32)
    pos_ref[0, 1:2, :] = jnp.sum(oh2 * c2, axis=0, keepdims=True).astype(I32)
    carry_s[...] = carry_s[...] + (tot1 + tot2)
    cnt_ref[...] = carry_s[...]


def _route_call(aff_t, b_router, tn):
    m = aff_t.shape[1]
    nt = m // tn
    tok = lambda dt: jax.ShapeDtypeStruct((nt, TOP_K, tn), dt)
    tok_spec = pl.BlockSpec((1, TOP_K, tn), lambda i: (i, 0, 0))
    return pl.pallas_call(
        _route_kernel,
        out_shape=(tok(I32), tok(F32), tok(I32), jax.ShapeDtypeStruct((N_EXPERTS, LANES), F32)),
        grid=(nt,),
        in_specs=[pl.BlockSpec((N_EXPERTS, tn), lambda i: (0, i)),
                  pl.BlockSpec((N_EXPERTS, 1), lambda i: (0, 0))],
        out_specs=(tok_spec, tok_spec, tok_spec, pl.BlockSpec((N_EXPERTS, LANES), lambda i: (0, 0))),
        scratch_shapes=[pltpu.VMEM((N_EXPERTS, LANES), F32), pltpu.VMEM((tn, tn), BF16)],
        compiler_params=_cparams("arbitrary"),
        name="route",
    )(aff_t, b_router.reshape(N_EXPERTS, 1))


def _dispatch_kernel(pstart_ref, eid_ref, pos_ref, h_ref, xs_in_ref, xs_ref, sem):
    del xs_in_ref
    tm = h_ref.shape[0]

    def row_copy(t, dst_row):
        return pltpu.make_async_copy(h_ref.at[pl.ds(t, 1)], xs_ref.at[pl.ds(dst_row, 1)], sem)

    def issue(t, c):
        for s in range(TOP_K):
            row_copy(t, pstart_ref[eid_ref[0, s, t]] + pos_ref[0, s, t]).start()
        return c

    def drain(t, c):
        for s in range(TOP_K):
            row_copy(0, 0).wait()
        return c

    lax.fori_loop(0, tm, issue, 0)
    lax.fori_loop(0, tm, drain, 0)


def _dispatch_call(pstart, eid, pos, h, n_rows):
    m, d = h.shape
    nt, _, tm = eid.shape
    smem_tok = pl.BlockSpec((1, TOP_K, tm), lambda i, ps: (i, 0, 0), memory_space=pltpu.SMEM)
    return pl.pallas_call(
        _dispatch_kernel,
        out_shape=jax.ShapeDtypeStruct((n_rows, d), F32),
        grid_spec=pltpu.PrefetchScalarGridSpec(
            num_scalar_prefetch=1,
            grid=(nt,),
            in_specs=[smem_tok, smem_tok, pl.BlockSpec((tm, d), lambda i, ps: (i, 0)),
                      pl.BlockSpec(memory_space=pl.ANY)],
            out_specs=pl.BlockSpec(memory_space=pl.ANY),
            scratch_shapes=[pltpu.SemaphoreType.DMA(())]),
        input_output_aliases={4: 0},
        compiler_params=_cparams("arbitrary"),
        name="dispatch",
    )(pstart, eid, pos, h, jnp.zeros((n_rows, d), F32))


def _expert_kernel(be_ref, nused_ref, xs_ref, wg_ref, wu_ref, wd_ref, ys_ref):
    j = pl.program_id(0)

    @pl.when(j < nused_ref[0])
    def _():
        x = xs_ref[...].astype(BF16)
        hid = _silu(_dot(x, wg_ref[0, 0])) * _dot(x, wu_ref[0, 0])
        ys_ref[...] = _dot(hid, wd_ref[0, 0])

    @pl.when(j >= nused_ref[0])
    def _():
        ys_ref[...] = jnp.zeros(ys_ref.shape, F32)


def _expert_call(block_e, n_used, xs, w_gate, w_up, w_down, layer):
    n_rows, d = xs.shape
    f = w_gate.shape[3]
    nb = n_rows // EXPERT_ROWS
    return pl.pallas_call(
        _expert_kernel,
        out_shape=jax.ShapeDtypeStruct((n_rows, d), F32),
        grid_spec=pltpu.PrefetchScalarGridSpec(
            num_scalar_prefetch=2,
            grid=(nb,),
            in_specs=[pl.BlockSpec((EXPERT_ROWS, d), lambda j, be, nu: (j, 0)),
                      pl.BlockSpec((1, 1, d, f), lambda j, be, nu: (layer, be[j], 0, 0)),
                      pl.BlockSpec((1, 1, d, f), lambda j, be, nu: (layer, be[j], 0, 0)),
                      pl.BlockSpec((1, 1, f, d), lambda j, be, nu: (layer, be[j], 0, 0))],
            out_specs=pl.BlockSpec((EXPERT_ROWS, d), lambda j, be, nu: (j, 0))),
        compiler_params=_cparams("arbitrary"),
        name="experts",
    )(block_e, n_used, xs, w_gate, w_up, w_down)


def _combine_kernel(pstart_ref, eid_ref, pos_ref, wt_ref, h_ref, ple_ref, ys_ref, g_ref, b_ref, o_ref,
                    buf, sem):
    tm = h_ref.shape[0]

    def row_copy(src_row, s, t):
        return pltpu.make_async_copy(ys_ref.at[pl.ds(src_row, 1)], buf.at[s, pl.ds(t, 1)], sem)

    def issue(t, c):
        for s in range(TOP_K):
            row_copy(pstart_ref[eid_ref[0, s, t]] + pos_ref[0, s, t], s, t).start()
        return c

    def drain(t, c):
        for s in range(TOP_K):
            row_copy(0, s, 0).wait()
        return c

    lax.fori_loop(0, tm, issue, 0)
    lax.fori_loop(0, tm, drain, 0)

    w_cols = jnp.concatenate([wt_ref[0], jnp.zeros((LANES - TOP_K, tm), F32)], axis=0).T
    ffn = buf[0] * w_cols[:, 0:1] + buf[1] * w_cols[:, 1:2]
    o_ref[...] = _layer_norm(DN_ALPHA * h_ref[...] + ffn + ple_ref[...], g_ref[...], b_ref[...])


def _combine_call(pstart, eid, pos, wt, h, ple, ys, ln_g, ln_b):
    m, d = h.shape
    nt, _, tm = eid.shape
    smem_tok = pl.BlockSpec((1, TOP_K, tm), lambda i, ps: (i, 0, 0), memory_space=pltpu.SMEM)
    rows = pl.BlockSpec((tm, d), lambda i, ps: (i, 0))
    vec = pl.BlockSpec((1, d), lambda i, ps: (0, 0))
    return pl.pallas_call(
        _combine_kernel,
        out_shape=jax.ShapeDtypeStruct((m, d), F32),
        grid_spec=pltpu.PrefetchScalarGridSpec(
            num_scalar_prefetch=1,
            grid=(nt,),
            in_specs=[smem_tok, smem_tok, pl.BlockSpec((1, TOP_K, tm), lambda i, ps: (i, 0, 0)),
                      rows, rows, pl.BlockSpec(memory_space=pl.ANY), vec, vec],
            out_specs=rows,
            scratch_shapes=[pltpu.VMEM((TOP_K, tm, d), F32), pltpu.SemaphoreType.DMA(())]),
        compiler_params=_cparams("arbitrary"),
        name="combine",
    )(pstart, eid, pos, wt, h, ple, ys, ln_g.reshape(1, d), ln_b.reshape(1, d))


SEQS_PER_STEP = SUBLANES


def _gdn_sample_kernel(new_ref, prev_ref, z_ref, ab_ref, w_ref, alog_ref, dtb_ref, ng_ref, s_ref,
                       y_ref, so_ref):
    w = w_ref[...]
    pre = prev_ref[0] * w[0:1]
    for j in range(1, GDN_CONV - 1):
        pre = pre + prev_ref[j] * w[j:j + 1]
    qkv = _silu(pre + new_ref[...] * w[GDN_CONV - 1:GDN_CONV])
    g_all, beta_all = _gdn_gates(ab_ref[...], alog_ref[...], dtb_ref[...])
    ng = ng_ref[...]
    z = z_ref[...]
    pad = jnp.zeros((LANES - SEQS_PER_STEP, GDN_DK), F32)
    for h in range(GDN_HEADS):
        q = _l2_normalize(qkv[:, h * GDN_DK:(h + 1) * GDN_DK]) * (GDN_DK ** -0.5)
        k = _l2_normalize(qkv[:, GDN_QK + h * GDN_DK:GDN_QK + (h + 1) * GDN_DK])
        v = qkv[:, 2 * GDN_QK + h * GDN_DV:2 * GDN_QK + (h + 1) * GDN_DV]
        eg = jnp.exp(g_all[:, h:h + 1])
        beta = beta_all[:, GDN_HEADS + h:GDN_HEADS + h + 1]
        qk = jnp.sum(q * k, axis=-1, keepdims=True)
        k_t = jnp.concatenate([k, pad], axis=0).T
        qg_t = jnp.concatenate([q * eg, pad], axis=0).T
        outs = []
        for i in range(SEQS_PER_STEP):
            s_mat = s_ref[0, i, h]
            k_col = k_t[:, i:i + 1]
            k_s = jnp.sum(k_col * s_mat, axis=0, keepdims=True)
            q_s = jnp.sum(qg_t[:, i:i + 1] * s_mat, axis=0, keepdims=True)
            v_new = beta[i:i + 1] * (v[i:i + 1] - eg[i:i + 1] * k_s)
            outs.append(q_s + qk[i:i + 1] * v_new)
            so_ref[i, h] = s_mat * eg[i:i + 1] + k_col * v_new
        o = jnp.concatenate(outs, axis=0)
        o = o * lax.rsqrt(jnp.mean(o * o, axis=-1, keepdims=True) + NORM_EPS) * ng
        y_ref[:, h * GDN_DV:(h + 1) * GDN_DV] = (o * _silu(z[:, h * GDN_DV:(h + 1) * GDN_DV])).astype(y_ref.dtype)


def _gdn_sample_call(conv_new, conv_prev_t, z, ab, conv_w, a_log, dt_bias, norm_g, state, layer):
    n, c = conv_new.shape
    sp = SEQS_PER_STEP
    alog_row = jnp.pad(a_log, (0, LANES - GDN_HEADS)).reshape(1, LANES)
    dtb_row = jnp.pad(dt_bias, (0, LANES - GDN_HEADS)).reshape(1, LANES)
    row_spec = pl.BlockSpec((1, LANES), lambda i: (0, 0))
    s_spec = pl.BlockSpec((sp, GDN_HEADS, GDN_DK, GDN_DV), lambda i: (i, 0, 0, 0))
    s_in_spec = pl.BlockSpec((1, sp, GDN_HEADS, GDN_DK, GDN_DV), lambda i: (layer, i, 0, 0, 0))
    return pl.pallas_call(
        _gdn_sample_kernel,
        out_shape=(jax.ShapeDtypeStruct((n, GDN_VW), BF16), jax.ShapeDtypeStruct(state.shape[1:], F32)),
        grid=(n // sp,),
        in_specs=[pl.BlockSpec((sp, c), lambda i: (i, 0)),
                  pl.BlockSpec((GDN_CONV - 1, sp, c), lambda i: (0, i, 0)),
                  pl.BlockSpec((sp, GDN_VW), lambda i: (i, 0)),
                  pl.BlockSpec((sp, LANES), lambda i: (i, 0)),
                  pl.BlockSpec((GDN_CONV, c), lambda i: (0, 0)),
                  row_spec, row_spec, row_spec, s_in_spec],
        out_specs=(pl.BlockSpec((sp, GDN_VW), lambda i: (i, 0)), s_spec),
        compiler_params=_cparams("parallel"),
        name="gdn_sample",
    )(conv_new, conv_prev_t, z, ab, conv_w, alog_row, dtb_row, norm_g.reshape(1, GDN_DV), state)


PAGES_PER_SEQ = PAST_LEN // PAGE_SIZE
PAGES_PER_BLOCK = MOBA_BLOCK // PAGE_SIZE
PAST_BLOCKS = PAST_LEN // MOBA_BLOCK


def _moba_sample_kernel(pt_ref, q_ref, qc_ref, kn_ref, vn_ref, rb_ref, *rest):
    del pt_ref
    np_ = PAGES_PER_SEQ
    k_pages = rest[:np_]
    v_pages = rest[np_:2 * np_]
    o_ref, bias_s = rest[2 * np_:]
    n = pl.program_id(0)
    nh, hd = MOBA_HEADS, MOBA_HD
    rb = rb_ref[...]
    lane_pos = lax.broadcasted_iota(I32, (nh, PAST_LEN), 1)

    @pl.when(n == 0)
    def _():
        dist = PAST_LEN - lane_pos
        b = jnp.broadcast_to(rb[:, 0:1], (nh, PAST_LEN))
        for kk in range(1, REL_BUCKETS):
            b = jnp.where(dist >= T5_THRESHOLDS[kk], rb[:, kk:kk + 1], b)
        bias_s[...] = b

    q = q_ref[0]
    q_cols = [jnp.broadcast_to(qc_ref[0, h], (hd, PAGE_SIZE)) for h in range(nh)]

    page_rows = []
    for j in range(np_):
        rows = [jnp.sum(k_pages[j][0, 0, h] * q_cols[h], axis=0, keepdims=True) for h in range(nh)]
        page_rows.append(jnp.concatenate(rows, axis=0))
    s_raw = jnp.concatenate(page_rows, axis=1)

    gates = [jnp.sum(s_raw[:, b * MOBA_BLOCK:(b + 1) * MOBA_BLOCK], axis=1, keepdims=True) * (1.0 / MOBA_BLOCK)
             for b in range(PAST_BLOCKS)]
    sel = jnp.zeros((nh, PAST_LEN), F32)
    for b in range(PAST_BLOCKS):
        rank = jnp.zeros((nh, 1), F32)
        for b2 in range(PAST_BLOCKS):
            if b2 != b:
                ahead = (gates[b2] > gates[b]) | (gates[b2] == gates[b]) if b2 < b else gates[b2] > gates[b]
                rank = rank + jnp.where(ahead, 1.0, 0.0)
        sel = jnp.where(lane_pos // MOBA_BLOCK == b, jnp.where(rank < MOBA_TOPK, 1.0, 0.0), sel)

    scale = hd ** -0.5
    s = jnp.where(sel > 0.0, s_raw * scale + bias_s[...], NEG_BIG)
    s_new = jnp.sum(q * kn_ref[0], axis=1, keepdims=True) * scale + rb[:, 0:1]
    m = jnp.maximum(jnp.max(s, axis=1, keepdims=True), s_new)
    p = jnp.exp(s - m)
    p_new = jnp.exp(s_new - m)
    inv_l = 1.0 / (jnp.sum(p, axis=1, keepdims=True) + p_new)
    pn = p * inv_l

    ones = jnp.ones((SUBLANES, PAGE_SIZE), F32)
    out_rows = []
    for h in range(nh):
        acc = jnp.zeros((hd, PAGE_SIZE), F32)
        for j in range(np_):
            acc = acc + pn[h:h + 1, j * PAGE_SIZE:(j + 1) * PAGE_SIZE] * v_pages[j][0, 0, h]
        out_rows.append(_dot_nt(ones, acc, precision=HIGHEST)[0:1])
    o_ref[0] = (jnp.concatenate(out_rows, axis=0) + (p_new * inv_l) * vn_ref[0]).astype(o_ref.dtype)


def _moba_sample_call(q, k_new, v_new, cache_k, cache_v, layer, page_table, rel_bias):
    n, nh, hd = q.shape
    np_ = PAGES_PER_SEQ
    rb = jnp.pad(rel_bias.T, ((0, 0), (0, LANES - REL_BUCKETS)))
    ck_t = cache_k.transpose(0, 1, 3, 4, 2)
    cv_t = cache_v.transpose(0, 1, 3, 4, 2)
    tok = pl.BlockSpec((1, nh, hd), lambda i, pt: (i, 0, 0))
    page = lambda j: pl.BlockSpec((1, 1, nh, hd, PAGE_SIZE),
                                  lambda i, pt: (layer, pt[i * np_ + j], 0, 0, 0))
    return pl.pallas_call(
        _moba_sample_kernel,
        out_shape=jax.ShapeDtypeStruct((n, nh, hd), BF16),
        grid_spec=pltpu.PrefetchScalarGridSpec(
            num_scalar_prefetch=1,
            grid=(n,),
            in_specs=[tok, pl.BlockSpec((1, nh, hd, 1), lambda i, pt: (i, 0, 0, 0)), tok, tok,
                      pl.BlockSpec((nh, LANES), lambda i, pt: (0, 0))]
            + [page(j) for j in range(np_)] + [page(j) for j in range(np_)],
            out_specs=tok,
            scratch_shapes=[pltpu.VMEM((nh, PAST_LEN), F32)]),
        compiler_params=_cparams("arbitrary"),
        name="moba_sample",
    )(page_table.reshape(-1), q, q.reshape(n, nh, hd, 1), k_new, v_new, rb,
      *([ck_t] * np_), *([cv_t] * np_))


def _moe_layout(counts_f32, n_rows):
    counts = counts_f32.astype(I32)
    padded = (counts + EXPERT_ROWS - 1) // EXPERT_ROWS * EXPERT_ROWS
    ends = jnp.cumsum(padded)
    pstart = ends - padded
    nb = n_rows // EXPERT_ROWS
    block_start = jnp.arange(nb, dtype=I32) * EXPERT_ROWS
    block_e = jnp.minimum(jnp.sum((ends[None, :] <= block_start[:, None]).astype(I32), axis=1), N_EXPERTS - 1)
    n_used = (ends[-1:] // EXPERT_ROWS).astype(I32)
    return pstart.astype(I32), block_e.astype(I32), n_used


def _moe_rows(m):
    n_asg = m * TOP_K
    return -(-(n_asg + N_EXPERTS * (EXPERT_ROWS - 1)) // EXPERT_ROWS) * EXPERT_ROWS


def _token_tail(ya, yb, gate, x, p, layer, lw, tm):
    h, ple, aff_t = _merge_call(ya, yb, gate, x, p, layer, lw['wug'], lw['wum'], lw['wo'], lw['wpg'],
                                lw['wpp'], lw['wrt'], lw['ln1_g'], lw['ln1_b'], tm)
    eid, wt, pos, cnt = _route_call(aff_t, lw['b_router'], tm)
    n_rows = _moe_rows(x.shape[0])
    pstart, block_e, n_used = _moe_layout(cnt[:, 0], n_rows)
    xs = _dispatch_call(pstart, eid, pos, h, n_rows)
    ys = _expert_call(block_e, n_used, xs, lw['w_exp_gate'], lw['w_exp_up'], lw['w_exp_down'], layer)
    return _combine_call(pstart, eid, pos, wt, h, ple, ys, lw['ln2_g'], lw['ln2_b'])


def kernel(x_prompt, x_sample, cache_k, cache_v, state_gdn, state_conv, page_table, p_prompt, p_sample,
           ln0_g, ln0_b, w_in, gdn_conv_w, gdn_a_log, gdn_dt_bias, gdn_norm_g, w_up_gdn, w_up_moba, w_o,
           ln1_g, ln1_b, rel_bias, w_router, b_router, w_exp_gate, w_exp_up, w_exp_down,
           w_ple_gate, w_ple_proj, ln2_g, ln2_b):
    assert GDN_DK == LANES and GDN_DV == LANES
    n_p, t_p, d = x_prompt.shape
    n_s, t_s, _ = x_sample.shape
    assert t_s == 1 and t_p % MOBA_BLOCK == 0 and n_s % SEQS_PER_STEP == 0
    m_p, m_s = n_p * t_p, n_s
    tm_p, tm_s = _row_tile(m_p, 256), _row_tile(m_s, 256)
    pp = p_prompt.reshape(DEPTH, m_p, PLE_DIM)
    ps = p_sample.reshape(DEPTH, m_s, PLE_DIM)
    heads = lambda a: a.reshape(a.shape[0], MOBA_HEADS, MOBA_HD)

    xp = _ln_call(x_prompt.reshape(m_p, d), ln0_g, ln0_b)
    xs = _ln_call(x_sample.reshape(m_s, d), ln0_g, ln0_b)
    wrt = w_router.T
    outs = {name: [] for name in ('kp', 'vp', 'sp', 'cp', 'ks', 'vs', 'ss', 'cs')}
    for i in range(DEPTH):
        w_parts = _split_w_in(w_in[i])
        lw = dict(wug=w_up_gdn[i].astype(BF16), wum=w_up_moba[i].astype(BF16), wo=w_o[i].astype(BF16),
                  wpg=w_ple_gate[i].astype(BF16), wpp=w_ple_proj[i].astype(BF16), wrt=wrt,
                  ln1_g=ln1_g[i], ln1_b=ln1_b[i], ln2_g=ln2_g[i], ln2_b=ln2_b[i], b_router=b_router,
                  w_exp_gate=w_exp_gate, w_exp_up=w_exp_up, w_exp_down=w_exp_down)

        conv, z, ab, q, gate, kt, vt = _proj_call(xp, w_parts, n_p)
        seq = lambda a: a.reshape(n_p, t_p, a.shape[-1])
        ya, s_new = _gdn_prompt_call(seq(conv), seq(z), seq(ab), gdn_conv_w[i], gdn_a_log[i],
                                     gdn_dt_bias[i], gdn_norm_g[i])
        yb = _moba_prompt_call(seq(q), kt, vt, rel_bias)
        xp = _token_tail(ya.reshape(m_p, GDN_VW), yb.reshape(m_p, MOBA_W), gate, xp, pp, i, lw, tm_p)
        untranspose = lambda a: a.reshape(n_p, MOBA_HEADS, MOBA_HD, t_p).transpose(0, 3, 1, 2)
        outs['kp'].append(untranspose(kt))
        outs['vp'].append(untranspose(vt))
        outs['sp'].append(s_new)
        outs['cp'].append(seq(conv)[:, t_p - (GDN_CONV - 1):, :])

        conv, z, ab, q, gate, kt, vt = _proj_call(xs, w_parts, 1)
        k = kt[0].T
        v = vt[0].T
        ya, s_new = _gdn_sample_call(conv, state_conv[i].transpose(1, 0, 2), z, ab, gdn_conv_w[i],
                                     gdn_a_log[i], gdn_dt_bias[i], gdn_norm_g[i], state_gdn, i)
        yb = _moba_sample_call(heads(q), heads(k), heads(v), cache_k, cache_v, i, page_table, rel_bias)
        xs = _token_tail(ya, yb.reshape(m_s, MOBA_W), gate, xs, ps, i, lw, tm_s)
        outs['ks'].append(k.reshape(n_s, 1, MOBA_HEADS, MOBA_HD))
        outs['vs'].append(v.reshape(n_s, 1, MOBA_HEADS, MOBA_HD))
        outs['ss'].append(s_new)
        outs['cs'].append(jnp.concatenate([state_conv[i][:, 1:, :], conv[:, None, :]], axis=1))

    st = lambda name: jnp.stack(outs[name])
    return (xp.reshape(n_p, t_p, d), xs.reshape(n_s, 1, d), st('kp'), st('vp'), st('sp'), st('cp'),
            st('ks'), st('vs'), st('ss'), st('cs'))
```

```python
import functools
import math

import numpy as np
import jax
import jax.numpy as jnp
from jax import lax
from jax.experimental import pallas as pl
from jax.experimental.pallas import tpu as pltpu

F32 = jnp.float32
BF16 = jnp.bfloat16
I32 = jnp.int32
HIGHEST = lax.Precision.HIGHEST

D_MODEL = 1024
DEPTH = 4
PAST_LEN = 2048
PAGE_SIZE = 128
GDN_HEADS = 4
GDN_DK = 128
GDN_DV = 128
GDN_CONV = 4
GDN_CHUNK = 64
MOBA_HEADS = 8
MOBA_HD = 64
MOBA_BLOCK = 256
MOBA_TOPK = 3
REL_BUCKETS = 32
REL_MAX_DIST = 128
N_EXPERTS = 32
N_GROUPS = 4
EXPERTS_PER_GROUP = N_EXPERTS // N_GROUPS
TOP_K = 2
D_FF_EXPERT = 512
PLE_DIM = 256
DN_ALPHA = (2 * DEPTH) ** 0.25
LN_EPS = 1e-5
NORM_EPS = 1e-6

GDN_QK = GDN_HEADS * GDN_DK
GDN_VW = GDN_HEADS * GDN_DV
GDN_CONV_CH = 2 * GDN_QK + GDN_VW
MOBA_W = MOBA_HEADS * MOBA_HD
COL_Z = GDN_CONV_CH
COL_A = COL_Z + GDN_VW
COL_B = COL_A + GDN_HEADS
COL_MOBA = COL_B + GDN_HEADS
COL_GATE = COL_MOBA + 3 * MOBA_W
N_IN = COL_GATE + 2 * D_MODEL

LANES = 128
SUBLANES = 8
VMEM_LIMIT_BYTES = 56 * 1024 * 1024

EXPERT_ROWS = 256
GDN_GROUP = 256
NEG_BIG = -1e30


def _cparams(*sem):
    return pltpu.CompilerParams(dimension_semantics=sem, vmem_limit_bytes=VMEM_LIMIT_BYTES)


def _row_tile(m, target):
    t = min(m, target)
    assert m % t == 0, (m, t)
    return t


def _t5_thresholds():
    exact = REL_BUCKETS // 2
    n = np.arange(0, 2 * REL_MAX_DIST)
    log_ratio = np.log(np.maximum(n, 1).astype(np.float32) / np.float32(exact)) / np.float32(
        math.log(REL_MAX_DIST / exact))
    large = exact + (log_ratio * np.float32(REL_BUCKETS - exact)).astype(np.int32)
    bucket = np.where(n < exact, n, np.minimum(large, REL_BUCKETS - 1))
    assert np.all(np.diff(bucket) >= 0)
    return [int(np.argmax(bucket >= k)) for k in range(REL_BUCKETS)]


T5_THRESHOLDS = _t5_thresholds()


def _layer_norm(x, g, b):
    mu = jnp.mean(x, axis=-1, keepdims=True)
    xc = x - mu
    var = jnp.mean(xc * xc, axis=-1, keepdims=True)
    return xc * lax.rsqrt(var + LN_EPS) * g + b


def _sigmoid(x):
    return 1.0 / (1.0 + jnp.exp(-x))


def _silu(x):
    return x * _sigmoid(x)


def _softplus(x):
    return jnp.maximum(x, 0.0) + jnp.log(1.0 + jnp.exp(-jnp.abs(x)))


def _dot(a, b):
    return jnp.dot(a.astype(BF16), b.astype(BF16), preferred_element_type=F32)


def _dot_nt(a, b, precision=None):
    if precision is None:
        a, b = a.astype(BF16), b.astype(BF16)
    return lax.dot_general(a, b, (((1,), (1,)), ((), ())), precision=precision,
                           preferred_element_type=F32)


def _dot_tn(a, b, precision=None):
    if precision is None:
        a, b = a.astype(BF16), b.astype(BF16)
    return lax.dot_general(a, b, (((0,), (0,)), ((), ())), precision=precision,
                           preferred_element_type=F32)


def _ln_kernel(x_ref, g_ref, b_ref, o_ref):
    o_ref[...] = _layer_norm(x_ref[...], g_ref[...], b_ref[...])


def _ln_call(x, g, b):
    m, d = x.shape
    tm = _row_tile(m, 512)
    return pl.pallas_call(
        _ln_kernel,
        out_shape=jax.ShapeDtypeStruct((m, d), F32),
        grid=(m // tm,),
        in_specs=[pl.BlockSpec((tm, d), lambda i: (i, 0)),
                  pl.BlockSpec((1, d), lambda i: (0, 0)),
                  pl.BlockSpec((1, d), lambda i: (0, 0))],
        out_specs=pl.BlockSpec((tm, d), lambda i: (i, 0)),
        compiler_params=_cparams("parallel"),
        name="ln0",
    )(x, g.reshape(1, d), b.reshape(1, d))


N_ROW_PARTS = 5


def _proj_kernel(x_ref, wc_ref, wz_ref, wab_ref, wq_ref, wg_ref, wkt_ref, wvt_ref,
                 conv_ref, z_ref, ab_ref, q_ref, gate_ref, kt_ref, vt_ref):
    x = x_ref[...].astype(BF16)
    for w_ref, o_ref in ((wc_ref, conv_ref), (wz_ref, z_ref), (wab_ref, ab_ref), (wq_ref, q_ref),
                         (wg_ref, gate_ref)):
        o_ref[...] = jnp.dot(x, w_ref[...], preferred_element_type=F32)
    kt_ref[0] = _dot_nt(wkt_ref[...], x)
    vt_ref[0] = _dot_nt(wvt_ref[...], x)


def _split_w_in(w):
    wab = jnp.pad(w[:, COL_A:COL_MOBA], ((0, 0), (0, LANES - 2 * GDN_HEADS)))
    parts = (w[:, :COL_Z], w[:, COL_Z:COL_A], wab, w[:, COL_MOBA:COL_MOBA + MOBA_W], w[:, COL_GATE:],
             w[:, COL_MOBA + MOBA_W:COL_MOBA + 2 * MOBA_W].T, w[:, COL_MOBA + 2 * MOBA_W:COL_GATE].T)
    return tuple(p.astype(BF16) for p in parts)


def _proj_call(x, w_parts, n_seq):
    m, d = x.shape
    t_len = m // n_seq
    tm = _row_tile(t_len, 256)
    tiles = t_len // tm
    row_w = [p.shape[1] for p in w_parts[:N_ROW_PARTS]]
    col_w = [p.shape[0] for p in w_parts[N_ROW_PARTS:]]
    return pl.pallas_call(
        _proj_kernel,
        out_shape=tuple(jax.ShapeDtypeStruct((m, n), F32) for n in row_w)
        + tuple(jax.ShapeDtypeStruct((n_seq, n, t_len), F32) for n in col_w),
        grid=(m // tm,),
        in_specs=[pl.BlockSpec((tm, d), lambda i: (i, 0))]
        + [pl.BlockSpec(p.shape, lambda i: (0, 0)) for p in w_parts],
        out_specs=tuple(pl.BlockSpec((tm, n), lambda i: (i, 0)) for n in row_w)
        + tuple(pl.BlockSpec((1, n, tm), lambda i: (i // tiles, 0, i % tiles)) for n in col_w),
        compiler_params=_cparams("parallel"),
        name="proj",
    )(x, *w_parts)


def _chunk_cumsum(x, pos_in_chunk, axis):
    s = 1
    while s < GDN_CHUNK:
        x = x + jnp.where(pos_in_chunk >= s, pltpu.roll(x, s, axis), 0.0)
        s *= 2
    return x


def _l2_normalize(x):
    return x * lax.rsqrt(jnp.sum(x * x, axis=-1, keepdims=True) + NORM_EPS)


def _gdn_gates(ab, alog_row, dtb_row):
    g = -jnp.exp(alog_row) * _softplus(ab + dtb_row)
    return g, _sigmoid(ab)


GDN_GROUPS_PER_ITER = 4


def _gdn_prompt_kernel(conv_ref, z_ref, ab_ref, w_ref, alog_ref, dtb_ref, ng_ref, y_ref, s_ref,
                       qn_s, kn_s, kb_s, vb_s, gcb_s, gcr_s, u_s, w_s, qg_s, kg_s, intra_s, egl_s):
    t_len = conv_ref.shape[1]
    n_groups = t_len // GDN_GROUP
    n_chunks = t_len // GDN_CHUNK
    cpg = GDN_GROUP // GDN_CHUNK
    groups_per_iter = math.gcd(n_groups, GDN_GROUPS_PER_ITER)

    row = lax.broadcasted_iota(I32, (t_len, LANES), 0)
    lane = lax.broadcasted_iota(I32, (t_len, LANES), 1)

    ab_t = ab_ref[0].T[0:SUBLANES]
    g_rows = -jnp.exp(alog_ref[:, 0:1]) * _softplus(ab_t + dtb_ref[:, 0:1])
    lane_t = lax.broadcasted_iota(I32, (SUBLANES, t_len), 1)
    sub_t = lax.broadcasted_iota(I32, (SUBLANES, t_len), 0)
    gc_rows = _chunk_cumsum(g_rows, lane_t % GDN_CHUNK, 1)
    packed = jnp.where(sub_t < GDN_HEADS, gc_rows, _sigmoid(ab_t))
    cols_all = jnp.concatenate([packed, jnp.zeros((LANES - SUBLANES, t_len), F32)], axis=0).T

    ii = lax.broadcasted_iota(I32, (GDN_GROUP, GDN_GROUP), 0)
    jj = lax.broadcasted_iota(I32, (GDN_GROUP, GDN_GROUP), 1)
    same = (ii // GDN_CHUNK) == (jj // GDN_CHUNK)
    incl = same & (ii >= jj)
    strict = same & (ii > jj)
    eye = (ii == jj).astype(F32)

    def conv(col0, h):
        cols = pl.ds(pl.multiple_of(col0 + h * LANES, LANES), LANES)
        x = conv_ref[0, :, cols]
        w = w_ref[:, cols]
        y = x * w[GDN_CONV - 1:GDN_CONV]
        for j in range(1, GDN_CONV):
            y = y + jnp.where(row >= j, pltpu.roll(x, j, 0), 0.0) * w[GDN_CONV - 1 - j:GDN_CONV - j]
        return _silu(y)

    def head_body(h, carry):
        qn = _l2_normalize(conv(0, h)) * (GDN_DK ** -0.5)
        kn = _l2_normalize(conv(GDN_QK, h))
        vv = conv(2 * GDN_QK, h)
        gc_col = jnp.sum(jnp.where(lane == h, cols_all, 0.0), axis=1, keepdims=True)
        beta_col = jnp.sum(jnp.where(lane == h + GDN_HEADS, cols_all, 0.0), axis=1, keepdims=True)
        gc_row = jnp.sum(jnp.where(sub_t == h, gc_rows, 0.0), axis=0, keepdims=True)
        qn_s[...] = qn
        kn_s[...] = kn
        kb_s[...] = kn * beta_col
        vb_s[...] = vv * beta_col
        gcb_s[...] = jnp.broadcast_to(gc_col, (t_len, LANES))
        for g in range(n_groups):
            gcr_s[g] = jnp.broadcast_to(gc_row[:, g * GDN_GROUP:(g + 1) * GDN_GROUP], (SUBLANES, GDN_GROUP))

        def group_iter(it, c2):
            gs = [it * groups_per_iter + gg for gg in range(groups_per_iter)]
            r0s = [pl.multiple_of(g * GDN_GROUP, GDN_GROUP) for g in gs]
            rows = [pl.ds(r0, GDN_GROUP) for r0 in r0s]
            q_g = [qn_s[r, :] for r in rows]
            k_g = [kn_s[r, :] for r in rows]
            kb_g = [kb_s[r, :] for r in rows]
            vb_g = [vb_s[r, :] for r in rows]
            gcb_g = [gcb_s[r, :] for r in rows]
            gcol = [x[:, 0:1] for x in gcb_g]
            eg = [jnp.exp(x) for x in gcol]
            decay = [jnp.where(incl, jnp.exp(jnp.where(incl, gc - gcr_s[g][0:1, :], 0.0)), 0.0)
                     for gc, g in zip(gcol, gs)]
            kk = [_dot_nt(a, b) for a, b in zip(kb_g, k_g)]
            qk = [_dot_nt(a, b) for a, b in zip(q_g, k_g)]
            a_mat = [jnp.where(strict, x * d, 0.0) for x, d in zip(kk, decay)]
            intra = [x * d for x, d in zip(qk, decay)]
            t_inv = [eye - a for a in a_mat]
            x_pow = a_mat
            p = 2
            while p < GDN_CHUNK:
                x_pow = [_dot(x, x) for x in x_pow]
                t_inv = [t + _dot(t, x) for t, x in zip(t_inv, x_pow)]
                p *= 2
            uw = [_dot(t, jnp.concatenate([vb, kb * e], axis=1)) for t, vb, kb, e in zip(t_inv, vb_g, kb_g, eg)]
            for i, g in enumerate(gs):
                u_s[h, rows[i], :] = uw[i][:, :GDN_DV]
                w_s[h, rows[i], :] = uw[i][:, GDN_DV:].astype(w_s.dtype)
                qg_s[h, rows[i], :] = (q_g[i] * eg[i]).astype(qg_s.dtype)
                for c in range(cpg):
                    lo = c * GDN_CHUNK
                    g_last = gcb_g[i][lo + GDN_CHUNK - 1:lo + GDN_CHUNK, :]
                    rows_c = pl.ds(pl.multiple_of(r0s[i] + lo, GDN_CHUNK), GDN_CHUNK)
                    kg = k_g[i][lo:lo + GDN_CHUNK] * jnp.exp(g_last - gcb_g[i][lo:lo + GDN_CHUNK])
                    kg_s[h, rows_c, :] = kg.astype(kg_s.dtype)
                    intra_s[h, rows_c, :] = intra[i][lo:lo + GDN_CHUNK, lo:lo + GDN_CHUNK].astype(intra_s.dtype)
                    egl_s[h, g * cpg + c] = jnp.broadcast_to(jnp.exp(g_last), (SUBLANES, LANES))
            return c2

        lax.fori_loop(0, n_groups // groups_per_iter, group_iter, 0)
        return carry

    lax.fori_loop(0, GDN_HEADS, head_body, 0)

    ng = ng_ref[...]

    def chunk_body(c, s_mats):
        r0 = pl.multiple_of(c * GDN_CHUNK, GDN_CHUNK)
        rows = pl.ds(r0, GDN_CHUNK)
        heads = range(GDN_HEADS)
        wq = [jnp.concatenate([w_s[h, rows, :], qg_s[h, rows, :]], axis=0) for h in heads]
        ws_qs = [_dot(wq[h], s_mats[h]) for h in heads]
        v_new = [u_s[h, rows, :] - ws_qs[h][:GDN_CHUNK] for h in heads]
        s_out = [s_mats[h] * egl_s[h, c][0:1, :] + _dot_tn(kg_s[h, rows, :], v_new[h]) for h in heads]
        o_all = [ws_qs[h][GDN_CHUNK:] + _dot(intra_s[h, rows, :], v_new[h]) for h in heads]
        for h in heads:
            cols = slice(h * GDN_DV, (h + 1) * GDN_DV)
            o = o_all[h]
            o = o * lax.rsqrt(jnp.mean(o * o, axis=-1, keepdims=True) + NORM_EPS) * ng * _silu(z_ref[0, rows, cols])
            y_ref[0, rows, cols] = o.astype(y_ref.dtype)
        return tuple(s_out)

    zero = jnp.zeros((GDN_DK, GDN_DV), F32)
    s_fin = lax.fori_loop(0, n_chunks, chunk_body, (zero,) * GDN_HEADS)
    for h in range(GDN_HEADS):
        s_ref[0, h] = s_fin[h]


def _gdn_prompt_call(conv, z, ab, conv_w, a_log, dt_bias, norm_g):
    n, t_len, c = conv.shape
    tile = lambda v: jnp.broadcast_to(jnp.pad(v, (0, SUBLANES - GDN_HEADS))[:, None], (SUBLANES, LANES))
    once = pl.Buffered(1)
    small = lambda shape: pl.BlockSpec(shape, lambda i: (0, 0))
    per_head = lambda w, dt: pltpu.VMEM((GDN_HEADS, t_len, w), dt)
    big = pltpu.VMEM((t_len, LANES), F32)
    return pl.pallas_call(
        _gdn_prompt_kernel,
        out_shape=(jax.ShapeDtypeStruct((n, t_len, GDN_VW), BF16),
                   jax.ShapeDtypeStruct((n, GDN_HEADS, GDN_DK, GDN_DV), F32)),
        grid=(n,),
        in_specs=[pl.BlockSpec((1, t_len, c), lambda i: (i, 0, 0), pipeline_mode=once),
                  pl.BlockSpec((1, t_len, GDN_VW), lambda i: (i, 0, 0), pipeline_mode=once),
                  pl.BlockSpec((1, t_len, LANES), lambda i: (i, 0, 0)),
                  small((GDN_CONV, c)), small((SUBLANES, LANES)), small((SUBLANES, LANES)),
                  small((1, GDN_DV))],
        out_specs=(pl.BlockSpec((1, t_len, GDN_VW), lambda i: (i, 0, 0)),
                   pl.BlockSpec((1, GDN_HEADS, GDN_DK, GDN_DV), lambda i: (i, 0, 0, 0))),
        scratch_shapes=[big, big, big, big, big,
                        pltpu.VMEM((t_len // GDN_GROUP, SUBLANES, GDN_GROUP), F32),
                        per_head(GDN_DV, F32), per_head(GDN_DK, BF16), per_head(GDN_DK, BF16),
                        per_head(GDN_DK, BF16), per_head(GDN_CHUNK, BF16),
                        pltpu.VMEM((GDN_HEADS, t_len // GDN_CHUNK, SUBLANES, LANES), F32)],
        compiler_params=_cparams("parallel"),
        name="gdn_prompt",
    )(conv, z, ab, conv_w, tile(a_log), tile(dt_bias), norm_g.reshape(1, GDN_DV))


HEADS_PER_TILE = LANES // MOBA_HD


def _t5_bias(dist, rb_ref, head):
    b = jnp.full(dist.shape, rb_ref[head, 0], F32)
    for k in range(1, REL_BUCKETS):
        b = jnp.where(dist >= T5_THRESHOLDS[k], rb_ref[head, k], b)
    return b


def _top_blocks(gate_t, n_valid):
    nq = gate_t.shape[1]
    sub = lax.broadcasted_iota(I32, gate_t.shape, 0)
    g = jnp.where(sub < n_valid, gate_t, -jnp.inf)
    rank = jnp.zeros(g.shape, F32)
    for b2 in range(SUBLANES):
        row = g[b2:b2 + 1, :]
        rank = rank + jnp.where(row > g, 1.0, 0.0) + jnp.where(row == g, jnp.where(sub > b2, 1.0, 0.0), 0.0)
    sel_t = jnp.where((rank < MOBA_TOPK) & (sub < n_valid), 1.0, 0.0)
    return jnp.concatenate([sel_t, jnp.zeros((LANES - SUBLANES, nq), F32)], axis=0).T


def _moba_prompt_kernel(rb_ref, q_ref, kt_ref, vt_ref, o_ref, kmean_s, biasd_s, biasp_s, logit_s, m_s,
                        acc_s):
    hp = pl.program_id(0)
    n = pl.program_id(1)
    qb = pl.program_id(2)
    blk = MOBA_BLOCK
    n_blocks = kt_ref.shape[2] // blk
    half = blk // 2
    ii = lax.broadcasted_iota(I32, (blk, blk), 0)
    jj = lax.broadcasted_iota(I32, (blk, blk), 1)
    lane = lax.broadcasted_iota(I32, (blk, LANES), 1)

    @pl.when((n == 0) & (qb == 0))
    def _():
        for e in range(HEADS_PER_TILE):
            head = hp * HEADS_PER_TILE + e
            biasd_s[e] = jnp.where(ii >= jj, _t5_bias(ii - jj, rb_ref, head), NEG_BIG)
            biasp_s[e] = _t5_bias(blk + ii - jj, rb_ref, head)

    @pl.when(qb == 0)
    def _():
        cols = [jnp.mean(kt_ref[0, :, b * blk:(b + 1) * blk], axis=1, keepdims=True) for b in range(n_blocks)]
        cols.append(jnp.zeros((LANES, LANES - n_blocks), F32))
        kmean_s[...] = jnp.concatenate(cols, axis=1).T

    q2 = q_ref[0]
    own_cols = pl.ds(pl.multiple_of(qb * blk, blk), blk)
    kt_own = kt_ref[0, :, own_cols].astype(BF16)
    scale = MOBA_HD ** -0.5

    q_scaled, sels, fars, m_init = [], [], [], []
    for e in range(HEADS_PER_TILE):
        qh = jnp.where(lane // MOBA_HD == e, q2, 0.0)
        gate_t = _dot_nt(kmean_s[0:SUBLANES, :], qh, precision=HIGHEST)
        sels.append(_top_blocks(gate_t, qb))
        fars.append(rb_ref[hp * HEADS_PER_TILE + e, REL_BUCKETS - 1])
        qs = (qh * scale).astype(BF16)
        q_scaled.append(qs)
        s = jnp.dot(qs, kt_own, preferred_element_type=F32) + biasd_s[e]
        logit_s[e, :, own_cols] = s
        m_init.append(jnp.maximum(s[:, :half], s[:, half:]))
    heads = range(HEADS_PER_TILE)

    def masked_logits(bs, tile_bias):
        cols = [pl.ds(pl.multiple_of(b * blk, blk), blk) for b in bs]
        kt_b = [kt_ref[0, :, c].astype(BF16) for c in cols]
        raw = [[jnp.dot(q_scaled[e], k, preferred_element_type=F32) for e in heads] for k in kt_b]
        picked = [[jnp.sum(jnp.where(lane == b, sels[e], 0.0), axis=-1, keepdims=True) for e in heads]
                  for b in bs]
        m_out = [None] * HEADS_PER_TILE
        for i in range(len(bs)):
            for e in heads:
                if tile_bias[e].ndim == 0:
                    s = raw[i][e] + jnp.where(picked[i][e] > 0.0, tile_bias[e], NEG_BIG)
                else:
                    s = raw[i][e] + tile_bias[e] + jnp.where(picked[i][e] > 0.0, 0.0, NEG_BIG)
                logit_s[e, :, cols[i]] = s
                m_blk = jnp.maximum(s[:, :half], s[:, half:])
                m_out[e] = m_blk if m_out[e] is None else jnp.maximum(m_out[e], m_blk)
        return m_out

    n_far = jnp.maximum(qb - 1, 0)

    def far_pair(it, m_parts):
        m_blk = masked_logits([2 * it, 2 * it + 1], fars)
        return tuple(jnp.maximum(m_parts[e], m_blk[e]) for e in heads)

    m_parts = lax.fori_loop(0, n_far // 2, far_pair, tuple(m_init))
    for e in heads:
        m_s[e] = m_parts[e]

    @pl.when(n_far % 2 == 1)
    def _():
        m_blk = masked_logits([n_far - 1], fars)
        for e in heads:
            m_s[e] = jnp.maximum(m_s[e], m_blk[e])

    @pl.when(qb >= 1)
    def _():
        m_blk = masked_logits([qb - 1], [biasp_s[e] for e in heads])
        for e in heads:
            m_s[e] = jnp.maximum(m_s[e], m_blk[e])

    m_rows = [jnp.max(m_s[e], axis=-1, keepdims=True) for e in heads]

    row = lax.broadcasted_iota(I32, (LANES, blk), 0)
    ones_row = [((e + 1) % HEADS_PER_TILE) * MOBA_HD for e in heads]

    def weigh_blocks(bs, accs):
        cols = [pl.ds(pl.multiple_of(b * blk, blk), blk) for b in bs]
        vt_b = [vt_ref[0, :, c] for c in cols]
        p = [[jnp.exp(logit_s[e, :, c] - m_rows[e]) for e in heads] for c in cols]
        vt_aug = [[jnp.where(row // MOBA_HD == e, v, jnp.where(row == ones_row[e], 1.0, 0.0)) for e in heads]
                  for v in vt_b]
        pv = [[_dot_nt(p[i][e], vt_aug[i][e]) for e in heads] for i in range(len(bs))]
        out = list(accs)
        for i in range(len(bs)):
            for e in heads:
                out[e] = out[e] + pv[i][e]
        return tuple(out)

    n_vis = qb + 1
    zero = jnp.zeros((blk, LANES), F32)
    accs = lax.fori_loop(0, n_vis // 2, lambda it, a: weigh_blocks([2 * it, 2 * it + 1], a),
                         (zero,) * HEADS_PER_TILE)
    for e in heads:
        acc_s[e] = accs[e]

    @pl.when(n_vis % 2 == 1)
    def _():
        last = weigh_blocks([qb], tuple(acc_s[e] for e in heads))
        for e in heads:
            acc_s[e] = last[e]

    fin = [acc_s[e] for e in heads]
    res = fin[0] / fin[0][:, ones_row[0]:ones_row[0] + 1]
    for e in range(1, HEADS_PER_TILE):
        res = jnp.where(lane // MOBA_HD == e, fin[e] / fin[e][:, ones_row[e]:ones_row[e] + 1], res)
    o_ref[0] = res.astype(o_ref.dtype)


def _moba_prompt_call(q, kt, vt, rel_bias):
    n, t_len, _ = q.shape
    blk = MOBA_BLOCK
    assert blk == 2 * LANES and t_len // blk <= SUBLANES
    n_tiles = MOBA_W // LANES
    kv_spec = pl.BlockSpec((1, LANES, t_len), lambda hp, i, qb, rb: (i, hp, 0))
    q_spec = pl.BlockSpec((1, blk, LANES), lambda hp, i, qb, rb: (i, qb, hp))
    return pl.pallas_call(
        _moba_prompt_kernel,
        out_shape=jax.ShapeDtypeStruct((n, t_len, MOBA_W), BF16),
        grid_spec=pltpu.PrefetchScalarGridSpec(
            num_scalar_prefetch=1,
            grid=(n_tiles, n, t_len // blk),
            in_specs=[q_spec, kv_spec, kv_spec],
            out_specs=q_spec,
            scratch_shapes=[pltpu.VMEM((LANES, LANES), F32),
                            pltpu.VMEM((HEADS_PER_TILE, blk, blk), F32),
                            pltpu.VMEM((HEADS_PER_TILE, blk, blk), F32),
                            pltpu.VMEM((HEADS_PER_TILE, blk, t_len), F32),
                            pltpu.VMEM((HEADS_PER_TILE, blk, LANES), F32),
                            pltpu.VMEM((HEADS_PER_TILE, blk, LANES), F32)]),
        compiler_params=_cparams("arbitrary", "arbitrary", "arbitrary"),
        name="moba_prompt",
    )(rel_bias.T, q, kt, vt)


MERGE_SUB_ROWS = 256


def _merge_kernel(ya_ref, yb_ref, gate_ref, x_ref, p_ref, wug_ref, wum_ref, wo_ref, wpg_ref, wpp_ref,
                  wrt_ref, g_ref, b_ref, h_ref, ple_ref, aff_ref):
    tm = x_ref.shape[0]
    sub = min(tm, MERGE_SUB_ROWS)
    parts = [slice(k * sub, (k + 1) * sub) for k in range(tm // sub)]
    up_a = [jnp.dot(ya_ref[r, :], wug_ref[...], preferred_element_type=F32) for r in parts]
    up_b = [jnp.dot(yb_ref[r, :], wum_ref[...], preferred_element_type=F32) for r in parts]
    merged = [_sigmoid(gate_ref[r, :D_MODEL]) * a + _sigmoid(gate_ref[r, D_MODEL:]) * b
              for r, a, b in zip(parts, up_a, up_b)]
    proj = [_dot(mg, wo_ref[...]) for mg in merged]
    h = [_layer_norm(DN_ALPHA * x_ref[r, :] + o, g_ref[...], b_ref[...]) for r, o in zip(parts, proj)]
    for r, hh in zip(parts, h):
        h_ref[r, :] = hh
    ple_gate = [_dot(hh, wpg_ref[...]) for hh in h]
    ple_val = [_dot(p_ref[0, r, :], wpp_ref[...]) for r in parts]
    for r, pg, pv in zip(parts, ple_gate, ple_val):
        ple_ref[r, :] = _sigmoid(pg) * pv
    for r, hh in zip(parts, h):
        aff_ref[:, r] = _sigmoid(_dot_nt(wrt_ref[...], hh, precision=HIGHEST))


def _merge_call(ya, yb, gate, x, p, layer, wug, wum, wo, wpg, wpp, wrt, ln_g, ln_b, tm):
    m, d = x.shape
    rows = lambda n: pl.BlockSpec((tm, n), lambda i: (i, 0))
    full = lambda a: pl.BlockSpec(a.shape, lambda i: (0, 0))
    return pl.pallas_call(
        _merge_kernel,
        out_shape=(jax.ShapeDtypeStruct((m, d), F32), jax.ShapeDtypeStruct((m, d), F32),
                   jax.ShapeDtypeStruct((N_EXPERTS, m), F32)),
        grid=(m // tm,),
        in_specs=[rows(GDN_VW), rows(MOBA_W), rows(2 * d), rows(d),
                  pl.BlockSpec((1, tm, PLE_DIM), lambda i: (layer, i, 0)),
                  full(wug), full(wum), full(wo), full(wpg), full(wpp), full(wrt),
                  pl.BlockSpec((1, d), lambda i: (0, 0)), pl.BlockSpec((1, d), lambda i: (0, 0))],
        out_specs=(rows(d), rows(d), pl.BlockSpec((N_EXPERTS, tm), lambda i: (0, i))),
        compiler_params=_cparams("parallel"),
        name="merge",
    )(ya, yb, gate, x, p, wug, wum, wo, wpg, wpp, wrt, ln_g.reshape(1, d), ln_b.reshape(1, d))


def _route_kernel(aff_ref, br_ref, eid_ref, wt_ref, pos_ref, cnt_ref, carry_s, tri_s):
    i = pl.program_id(0)
    tn = aff_ref.shape[1]

    @pl.when(i == 0)
    def _():
        carry_s[...] = jnp.zeros(carry_s.shape, F32)
        r = lax.broadcasted_iota(I32, (tn, tn), 0)
        c = lax.broadcasted_iota(I32, (tn, tn), 1)
        tri_s[...] = jnp.where(r < c, 1.0, 0.0).astype(tri_s.dtype)

    aff = aff_ref[...]
    sel = aff + br_ref[...]
    gsz = EXPERTS_PER_GROUP
    sub = lax.broadcasted_iota(I32, (gsz, tn), 0)
    best = None
    for g in range(N_GROUPS):
        v = sel[g * gsz:(g + 1) * gsz]
        a = aff[g * gsz:(g + 1) * gsz]
        m1 = jnp.max(v, axis=0, keepdims=True)
        i1 = jnp.min(jnp.where(v == m1, sub, gsz), axis=0, keepdims=True)
        v2 = jnp.where(sub == i1, -jnp.inf, v)
        m2 = jnp.max(v2, axis=0, keepdims=True)
        i2 = jnp.min(jnp.where(v2 == m2, sub, gsz), axis=0, keepdims=True)
        a1 = jnp.sum(jnp.where(sub == i1, a, 0.0), axis=0, keepdims=True)
        a2 = jnp.sum(jnp.where(sub == i2, a, 0.0), axis=0, keepdims=True)
        cand = (m1 + m2, i1 + g * gsz, i2 + g * gsz, a1, a2)
        if best is None:
            best = cand
        else:
            better = cand[0] > best[0]
            best = tuple(jnp.where(better, c, b) for c, b in zip(cand, best))
    _, e1, e2, a1, a2 = best
    eid_ref[0, 0:1, :] = e1
    eid_ref[0, 1:2, :] = e2
    wt_ref[0, 0:1, :] = a1 / (a1 + a2)
    wt_ref[0, 1:2, :] = a2 / (a1 + a2)

    e_iota = lax.broadcasted_iota(I32, (N_EXPERTS, tn), 0)
    oh1 = jnp.where(e_iota == e1, 1.0, 0.0)
    oh2 = jnp.where(e_iota == e2, 1.0, 0.0)
    tri = tri_s[...]
    tot1 = jnp.sum(oh1, axis=1, keepdims=True)
    tot2 = jnp.sum(oh2, axis=1, keepdims=True)
    base = carry_s[:, 0:1]
    c1 = base + jnp.dot(oh1.astype(tri.dtype), tri, preferred_element_type=F32)
    c2 = base + tot1 + jnp.dot(oh2.astype(tri.dtype), tri, preferred_element_type=F32)
    pos_ref[0, 0:1, :] = jnp.sum(oh1 * c1, axis=0, keepdims=True).astype(I32)
    pos_ref[0, 1:2, :] = jnp.sum(oh2 * c2, axis=0, keepdims=True).astype(I32)
    carry_s[...] = carry_s[...] + (tot1 + tot2)
    cnt_ref[...] = carry_s[...]


def _route_call(aff_t, b_router, tn):
    m = aff_t.shape[1]
    nt = m // tn
    tok = lambda dt: jax.ShapeDtypeStruct((nt, TOP_K, tn), dt)
    tok_spec = pl.BlockSpec((1, TOP_K, tn), lambda i: (i, 0, 0))
    return pl.pallas_call(
        _route_kernel,
        out_shape=(tok(I32), tok(F32), tok(I32), jax.ShapeDtypeStruct((N_EXPERTS, LANES), F32)),
        grid=(nt,),
        in_specs=[pl.BlockSpec((N_EXPERTS, tn), lambda i: (0, i)),
                  pl.BlockSpec((N_EXPERTS, 1), lambda i: (0, 0))],
        out_specs=(tok_spec, tok_spec, tok_spec, pl.BlockSpec((N_EXPERTS, LANES), lambda i: (0, 0))),
        scratch_shapes=[pltpu.VMEM((N_EXPERTS, LANES), F32), pltpu.VMEM((tn, tn), BF16)],
        compiler_params=_cparams("arbitrary"),
        name="route",
    )(aff_t, b_router.reshape(N_EXPERTS, 1))


ROW_DMA_UNROLL = 8


def _dest_kernel(pstart_ref, eid_ref, pos_ref, dest_ref):
    eid = eid_ref[0]
    dest = pos_ref[0]
    for e in range(N_EXPERTS):
        dest = dest + jnp.where(eid == e, pstart_ref[e], 0)
    dest_ref[0] = dest


def _dest_call(pstart, eid, pos):
    nt, _, tm = eid.shape
    tok = pl.BlockSpec((1, TOP_K, tm), lambda i, ps: (i, 0, 0))
    return pl.pallas_call(
        _dest_kernel,
        out_shape=jax.ShapeDtypeStruct(eid.shape, I32),
        grid_spec=pltpu.PrefetchScalarGridSpec(num_scalar_prefetch=1, grid=(nt,), in_specs=[tok, tok],
                                               out_specs=tok),
        compiler_params=_cparams("parallel"),
        name="dest",
    )(pstart, eid, pos)


def _dispatch_kernel(dest_ref, h_ref, xs_in_ref, xs_ref, sem):
    del xs_in_ref
    tm = h_ref.shape[0]

    def issue(it, c):
        for u in range(ROW_DMA_UNROLL):
            t = it * ROW_DMA_UNROLL + u
            for s in range(TOP_K):
                pltpu.make_async_copy(h_ref.at[pl.ds(t, 1)], xs_ref.at[pl.ds(dest_ref[0, s, t], 1)], sem).start()
        return c

    lax.fori_loop(0, tm // ROW_DMA_UNROLL, issue, 0)
    for s in range(TOP_K):
        pltpu.make_async_copy(h_ref, xs_ref.at[pl.ds(0, tm)], sem).wait()


def _dispatch_call(dest, h, n_rows):
    m, d = h.shape
    nt, _, tm = dest.shape
    assert tm % ROW_DMA_UNROLL == 0
    return pl.pallas_call(
        _dispatch_kernel,
        out_shape=jax.ShapeDtypeStruct((n_rows, d), F32),
        grid=(nt,),
        in_specs=[pl.BlockSpec((1, TOP_K, tm), lambda i: (i, 0, 0), memory_space=pltpu.SMEM),
                  pl.BlockSpec((tm, d), lambda i: (i, 0)),
                  pl.BlockSpec(memory_space=pl.ANY)],
        out_specs=pl.BlockSpec(memory_space=pl.ANY),
        scratch_shapes=[pltpu.SemaphoreType.DMA(())],
        input_output_aliases={2: 0},
        compiler_params=_cparams("arbitrary"),
        name="dispatch",
    )(dest, h, jnp.zeros((n_rows, d), F32))


EXPERT_FF_SPLIT = 2


def _expert_kernel(be_ref, nused_ref, xs_ref, wg_ref, wu_ref, wd_ref, ys_ref, wg_s, wu_s, wd_s):
    j = pl.program_id(0)

    @pl.when(j < nused_ref[0])
    def _():
        @pl.when((j == 0) | (be_ref[j] != be_ref[jnp.maximum(j - 1, 0)]))
        def _():
            wg_s[...] = wg_ref[0, 0].astype(wg_s.dtype)
            wu_s[...] = wu_ref[0, 0].astype(wu_s.dtype)
            wd_s[...] = wd_ref[0, 0].astype(wd_s.dtype)

        x = xs_ref[...].astype(BF16)
        fw = wg_s.shape[1] // EXPERT_FF_SPLIT
        cuts = [slice(k * fw, (k + 1) * fw) for k in range(EXPERT_FF_SPLIT)]
        gate = [_dot(x, wg_s[:, c]) for c in cuts]
        up = [_dot(x, wu_s[:, c]) for c in cuts]
        hid = [_silu(g) * u for g, u in zip(gate, up)]
        down = [_dot(hh, wd_s[c, :]) for hh, c in zip(hid, cuts)]
        ys_ref[...] = functools.reduce(lambda a, b: a + b, down)

    @pl.when(j >= nused_ref[0])
    def _():
        ys_ref[...] = jnp.zeros(ys_ref.shape, F32)


def _expert_call(block_e, n_used, xs, w_gate, w_up, w_down, layer):
    n_rows, d = xs.shape
    f = w_gate.shape[3]
    nb = n_rows // EXPERT_ROWS
    return pl.pallas_call(
        _expert_kernel,
        out_shape=jax.ShapeDtypeStruct((n_rows, d), F32),
        grid_spec=pltpu.PrefetchScalarGridSpec(
            num_scalar_prefetch=2,
            grid=(nb,),
            in_specs=[pl.BlockSpec((EXPERT_ROWS, d), lambda j, be, nu: (j, 0)),
                      pl.BlockSpec((1, 1, d, f), lambda j, be, nu: (layer, be[j], 0, 0)),
                      pl.BlockSpec((1, 1, d, f), lambda j, be, nu: (layer, be[j], 0, 0)),
                      pl.BlockSpec((1, 1, f, d), lambda j, be, nu: (layer, be[j], 0, 0))],
            out_specs=pl.BlockSpec((EXPERT_ROWS, d), lambda j, be, nu: (j, 0)),
            scratch_shapes=[pltpu.VMEM((d, f), BF16), pltpu.VMEM((d, f), BF16), pltpu.VMEM((f, d), BF16)]),
        compiler_params=_cparams("arbitrary"),
        name="experts",
    )(block_e, n_used, xs, w_gate, w_up, w_down)


def _combine_kernel(dest_ref, wt_ref, h_ref, ple_ref, ys_ref, g_ref, b_ref, o_ref, buf, sem):
    tm = h_ref.shape[0]

    def issue(it, c):
        for u in range(ROW_DMA_UNROLL):
            t = it * ROW_DMA_UNROLL + u
            for s in range(TOP_K):
                pltpu.make_async_copy(ys_ref.at[pl.ds(dest_ref[0, s, t], 1)], buf.at[s, pl.ds(t, 1)], sem).start()
        return c

    lax.fori_loop(0, tm // ROW_DMA_UNROLL, issue, 0)
    for s in range(TOP_K):
        pltpu.make_async_copy(ys_ref.at[pl.ds(0, tm)], buf.at[s], sem).wait()

    w_cols = jnp.concatenate([wt_ref[0], jnp.zeros((LANES - TOP_K, tm), F32)], axis=0).T
    ffn = buf[0] * w_cols[:, 0:1] + buf[1] * w_cols[:, 1:2]
    o_ref[...] = _layer_norm(DN_ALPHA * h_ref[...] + ffn + ple_ref[...], g_ref[...], b_ref[...])


def _combine_call(dest, wt, h, ple, ys, ln_g, ln_b):
    m, d = h.shape
    nt, _, tm = dest.shape
    assert tm % ROW_DMA_UNROLL == 0
    tok = lambda **kw: pl.BlockSpec((1, TOP_K, tm), lambda i: (i, 0, 0), **kw)
    rows = pl.BlockSpec((tm, d), lambda i: (i, 0))
    vec = pl.BlockSpec((1, d), lambda i: (0, 0))
    return pl.pallas_call(
        _combine_kernel,
        out_shape=jax.ShapeDtypeStruct((m, d), F32),
        grid=(nt,),
        in_specs=[tok(memory_space=pltpu.SMEM), tok(), rows, rows, pl.BlockSpec(memory_space=pl.ANY), vec, vec],
        out_specs=rows,
        scratch_shapes=[pltpu.VMEM((TOP_K, tm, d), F32), pltpu.SemaphoreType.DMA(())],
        compiler_params=_cparams("arbitrary"),
        name="combine",
    )(dest, wt, h, ple, ys, ln_g.reshape(1, d), ln_b.reshape(1, d))


SEQS_PER_STEP = SUBLANES


def _gdn_sample_kernel(new_ref, prev_ref, z_ref, ab_ref, w_ref, alog_ref, dtb_ref, ng_ref, s_ref,
                       y_ref, so_ref):
    w = w_ref[...]
    pre = prev_ref[0] * w[0:1]
    for j in range(1, GDN_CONV - 1):
        pre = pre + prev_ref[j] * w[j:j + 1]
    qkv = _silu(pre + new_ref[...] * w[GDN_CONV - 1:GDN_CONV])
    g_all, beta_all = _gdn_gates(ab_ref[...], alog_ref[...], dtb_ref[...])
    ng = ng_ref[...]
    z = z_ref[...]
    pad = jnp.zeros((LANES - SEQS_PER_STEP, GDN_DK), F32)
    for h in range(GDN_HEADS):
        q = _l2_normalize(qkv[:, h * GDN_DK:(h + 1) * GDN_DK]) * (GDN_DK ** -0.5)
        k = _l2_normalize(qkv[:, GDN_QK + h * GDN_DK:GDN_QK + (h + 1) * GDN_DK])
        v = qkv[:, 2 * GDN_QK + h * GDN_DV:2 * GDN_QK + (h + 1) * GDN_DV]
        eg = jnp.exp(g_all[:, h:h + 1])
        beta = beta_all[:, GDN_HEADS + h:GDN_HEADS + h + 1]
        qk = jnp.sum(q * k, axis=-1, keepdims=True)
        k_t = jnp.concatenate([k, pad], axis=0).T
        qg_t = jnp.concatenate([q * eg, pad], axis=0).T
        outs = []
        for i in range(SEQS_PER_STEP):
            s_mat = s_ref[0, i, h]
            k_col = k_t[:, i:i + 1]
            k_s = jnp.sum(k_col * s_mat, axis=0, keepdims=True)
            q_s = jnp.sum(qg_t[:, i:i + 1] * s_mat, axis=0, keepdims=True)
            v_new = beta[i:i + 1] * (v[i:i + 1] - eg[i:i + 1] * k_s)
            outs.append(q_s + qk[i:i + 1] * v_new)
            so_ref[i, h] = s_mat * eg[i:i + 1] + k_col * v_new
        o = jnp.concatenate(outs, axis=0)
        o = o * lax.rsqrt(jnp.mean(o * o, axis=-1, keepdims=True) + NORM_EPS) * ng
        y_ref[:, h * GDN_DV:(h + 1) * GDN_DV] = (o * _silu(z[:, h * GDN_DV:(h + 1) * GDN_DV])).astype(y_ref.dtype)


def _gdn_sample_call(conv_new, conv_prev_t, z, ab, conv_w, a_log, dt_bias, norm_g, state, layer):
    n, c = conv_new.shape
    sp = SEQS_PER_STEP
    alog_row = jnp.pad(a_log, (0, LANES - GDN_HEADS)).reshape(1, LANES)
    dtb_row = jnp.pad(dt_bias, (0, LANES - GDN_HEADS)).reshape(1, LANES)
    row_spec = pl.BlockSpec((1, LANES), lambda i: (0, 0))
    s_spec = pl.BlockSpec((sp, GDN_HEADS, GDN_DK, GDN_DV), lambda i: (i, 0, 0, 0))
    s_in_spec = pl.BlockSpec((1, sp, GDN_HEADS, GDN_DK, GDN_DV), lambda i: (layer, i, 0, 0, 0))
    return pl.pallas_call(
        _gdn_sample_kernel,
        out_shape=(jax.ShapeDtypeStruct((n, GDN_VW), BF16), jax.ShapeDtypeStruct(state.shape[1:], F32)),
        grid=(n // sp,),
        in_specs=[pl.BlockSpec((sp, c), lambda i: (i, 0)),
                  pl.BlockSpec((GDN_CONV - 1, sp, c), lambda i: (0, i, 0)),
                  pl.BlockSpec((sp, GDN_VW), lambda i: (i, 0)),
                  pl.BlockSpec((sp, LANES), lambda i: (i, 0)),
                  pl.BlockSpec((GDN_CONV, c), lambda i: (0, 0)),
                  row_spec, row_spec, row_spec, s_in_spec],
        out_specs=(pl.BlockSpec((sp, GDN_VW), lambda i: (i, 0)), s_spec),
        compiler_params=_cparams("parallel"),
        name="gdn_sample",
    )(conv_new, conv_prev_t, z, ab, conv_w, alog_row, dtb_row, norm_g.reshape(1, GDN_DV), state)


PAGES_PER_SEQ = PAST_LEN // PAGE_SIZE
PAGES_PER_BLOCK = MOBA_BLOCK // PAGE_SIZE
PAST_BLOCKS = PAST_LEN // MOBA_BLOCK


def _moba_sample_kernel(pt_ref, q_ref, qc_ref, kn_ref, vn_ref, rb_ref, *rest):
    del pt_ref
    np_ = PAGES_PER_SEQ
    k_pages = rest[:np_]
    v_pages = rest[np_:2 * np_]
    o_ref, bias_s = rest[2 * np_:]
    n = pl.program_id(0)
    nh, hd = MOBA_HEADS, MOBA_HD
    rb = rb_ref[...]
    lane_pos = lax.broadcasted_iota(I32, (nh, PAST_LEN), 1)

    @pl.when(n == 0)
    def _():
        dist = PAST_LEN - lane_pos
        b = jnp.broadcast_to(rb[:, 0:1], (nh, PAST_LEN))
        for kk in range(1, REL_BUCKETS):
            b = jnp.where(dist >= T5_THRESHOLDS[kk], rb[:, kk:kk + 1], b)
        bias_s[...] = b

    q = q_ref[0]
    q_cols = [jnp.broadcast_to(qc_ref[0, h], (hd, PAGE_SIZE)) for h in range(nh)]

    page_rows = []
    for j in range(np_):
        rows = [jnp.sum(k_pages[j][0, 0, h] * q_cols[h], axis=0, keepdims=True) for h in range(nh)]
        page_rows.append(jnp.concatenate(rows, axis=0))
    s_raw = jnp.concatenate(page_rows, axis=1)

    gates = [jnp.sum(s_raw[:, b * MOBA_BLOCK:(b + 1) * MOBA_BLOCK], axis=1, keepdims=True) * (1.0 / MOBA_BLOCK)
             for b in range(PAST_BLOCKS)]
    sel = jnp.zeros((nh, PAST_LEN), F32)
    for b in range(PAST_BLOCKS):
        rank = jnp.zeros((nh, 1), F32)
        for b2 in range(PAST_BLOCKS):
            if b2 != b:
                ahead = (gates[b2] > gates[b]) | (gates[b2] == gates[b]) if b2 < b else gates[b2] > gates[b]
                rank = rank + jnp.where(ahead, 1.0, 0.0)
        sel = jnp.where(lane_pos // MOBA_BLOCK == b, jnp.where(rank < MOBA_TOPK, 1.0, 0.0), sel)

    scale = hd ** -0.5
    s = jnp.where(sel > 0.0, s_raw * scale + bias_s[...], NEG_BIG)
    s_new = jnp.sum(q * kn_ref[0], axis=1, keepdims=True) * scale + rb[:, 0:1]
    m = jnp.maximum(jnp.max(s, axis=1, keepdims=True), s_new)
    p = jnp.exp(s - m)
    p_new = jnp.exp(s_new - m)
    inv_l = 1.0 / (jnp.sum(p, axis=1, keepdims=True) + p_new)
    pn = p * inv_l

    ones = jnp.ones((SUBLANES, PAGE_SIZE), F32)
    out_rows = []
    for h in range(nh):
        acc = jnp.zeros((hd, PAGE_SIZE), F32)
        for j in range(np_):
            acc = acc + pn[h:h + 1, j * PAGE_SIZE:(j + 1) * PAGE_SIZE] * v_pages[j][0, 0, h]
        out_rows.append(_dot_nt(ones, acc, precision=HIGHEST)[0:1])
    o_ref[0] = (jnp.concatenate(out_rows, axis=0) + (p_new * inv_l) * vn_ref[0]).astype(o_ref.dtype)


def _moba_sample_call(q, k_new, v_new, cache_k, cache_v, layer, page_table, rel_bias):
    n, nh, hd = q.shape
    np_ = PAGES_PER_SEQ
    rb = jnp.pad(rel_bias.T, ((0, 0), (0, LANES - REL_BUCKETS)))
    ck_t = cache_k.transpose(0, 1, 3, 4, 2)
    cv_t = cache_v.transpose(0, 1, 3, 4, 2)
    tok = pl.BlockSpec((1, nh, hd), lambda i, pt: (i, 0, 0))
    page = lambda j: pl.BlockSpec((1, 1, nh, hd, PAGE_SIZE),
                                  lambda i, pt: (layer, pt[i * np_ + j], 0, 0, 0))
    return pl.pallas_call(
        _moba_sample_kernel,
        out_shape=jax.ShapeDtypeStruct((n, nh, hd), BF16),
        grid_spec=pltpu.PrefetchScalarGridSpec(
            num_scalar_prefetch=1,
            grid=(n,),
            in_specs=[tok, pl.BlockSpec((1, nh, hd, 1), lambda i, pt: (i, 0, 0, 0)), tok, tok,
                      pl.BlockSpec((nh, LANES), lambda i, pt: (0, 0))]
            + [page(j) for j in range(np_)] + [page(j) for j in range(np_)],
            out_specs=tok,
            scratch_shapes=[pltpu.VMEM((nh, PAST_LEN), F32)]),
        compiler_params=_cparams("arbitrary"),
        name="moba_sample",
    )(page_table.reshape(-1), q, q.reshape(n, nh, hd, 1), k_new, v_new, rb,
      *([ck_t] * np_), *([cv_t] * np_))


def _moe_layout(counts_f32, n_rows):
    counts = counts_f32.astype(I32)
    padded = (counts + EXPERT_ROWS - 1) // EXPERT_ROWS * EXPERT_ROWS
    ends = jnp.cumsum(padded)
    pstart = ends - padded
    nb = n_rows // EXPERT_ROWS
    block_start = jnp.arange(nb, dtype=I32) * EXPERT_ROWS
    block_e = jnp.minimum(jnp.sum((ends[None, :] <= block_start[:, None]).astype(I32), axis=1), N_EXPERTS - 1)
    n_used = (ends[-1:] // EXPERT_ROWS).astype(I32)
    return pstart.astype(I32), block_e.astype(I32), n_used


def _moe_rows(m):
    n_asg = m * TOP_K
    return -(-(n_asg + N_EXPERTS * (EXPERT_ROWS - 1)) // EXPERT_ROWS) * EXPERT_ROWS


def _token_tail(ya, yb, gate, x, p, layer, lw, tm):
    h, ple, aff_t = _merge_call(ya, yb, gate, x, p, layer, lw['wug'], lw['wum'], lw['wo'], lw['wpg'],
                                lw['wpp'], lw['wrt'], lw['ln1_g'], lw['ln1_b'],
                                _row_tile(x.shape[0], 2 * MERGE_SUB_ROWS))
    eid, wt, pos, cnt = _route_call(aff_t, lw['b_router'], tm)
    n_rows = _moe_rows(x.shape[0])
    pstart, block_e, n_used = _moe_layout(cnt[:, 0], n_rows)
    dest = _dest_call(pstart, eid, pos)
    xs = _dispatch_call(dest, h, n_rows)
    ys = _expert_call(block_e, n_used, xs, lw['w_exp_gate'], lw['w_exp_up'], lw['w_exp_down'], layer)
    return _combine_call(dest, wt, h, ple, ys, lw['ln2_g'], lw['ln2_b'])


def kernel(x_prompt, x_sample, cache_k, cache_v, state_gdn, state_conv, page_table, p_prompt, p_sample,
           ln0_g, ln0_b, w_in, gdn_conv_w, gdn_a_log, gdn_dt_bias, gdn_norm_g, w_up_gdn, w_up_moba, w_o,
           ln1_g, ln1_b, rel_bias, w_router, b_router, w_exp_gate, w_exp_up, w_exp_down,
           w_ple_gate, w_ple_proj, ln2_g, ln2_b):
    assert GDN_DK == LANES and GDN_DV == LANES
    n_p, t_p, d = x_prompt.shape
    n_s, t_s, _ = x_sample.shape
    assert t_s == 1 and t_p % MOBA_BLOCK == 0 and n_s % SEQS_PER_STEP == 0
    m_p, m_s = n_p * t_p, n_s
    tm_p, tm_s = _row_tile(m_p, 256), _row_tile(m_s, 256)
    pp = p_prompt.reshape(DEPTH, m_p, PLE_DIM)
    ps = p_sample.reshape(DEPTH, m_s, PLE_DIM)
    heads = lambda a: a.reshape(a.shape[0], MOBA_HEADS, MOBA_HD)

    xp = _ln_call(x_prompt.reshape(m_p, d), ln0_g, ln0_b)
    xs = _ln_call(x_sample.reshape(m_s, d), ln0_g, ln0_b)
    wrt = w_router.T
    outs = {name: [] for name in ('kp', 'vp', 'sp', 'cp', 'ks', 'vs', 'ss', 'cs')}
    for i in range(DEPTH):
        w_parts = _split_w_in(w_in[i])
        lw = dict(wug=w_up_gdn[i].astype(BF16), wum=w_up_moba[i].astype(BF16), wo=w_o[i].astype(BF16),
                  wpg=w_ple_gate[i].astype(BF16), wpp=w_ple_proj[i].astype(BF16), wrt=wrt,
                  ln1_g=ln1_g[i], ln1_b=ln1_b[i], ln2_g=ln2_g[i], ln2_b=ln2_b[i], b_router=b_router,
                  w_exp_gate=w_exp_gate, w_exp_up=w_exp_up, w_exp_down=w_exp_down)

        conv, z, ab, q, gate, kt, vt = _proj_call(xp, w_parts, n_p)
        seq = lambda a: a.reshape(n_p, t_p, a.shape[-1])
        ya, s_new = _gdn_prompt_call(seq(conv), seq(z), seq(ab), gdn_conv_w[i], gdn_a_log[i],
                                     gdn_dt_bias[i], gdn_norm_g[i])
        yb = _moba_prompt_call(seq(q), kt, vt, rel_bias)
        xp = _token_tail(ya.reshape(m_p, GDN_VW), yb.reshape(m_p, MOBA_W), gate, xp, pp, i, lw, tm_p)
        untranspose = lambda a: a.reshape(n_p, MOBA_HEADS, MOBA_HD, t_p).transpose(0, 3, 1, 2)
        outs['kp'].append(untranspose(kt))
        outs['vp'].append(untranspose(vt))
        outs['sp'].append(s_new)
        outs['cp'].append(seq(conv)[:, t_p - (GDN_CONV - 1):, :])

        conv, z, ab, q, gate, kt, vt = _proj_call(xs, w_parts, 1)
        k = kt[0].T
        v = vt[0].T
        ya, s_new = _gdn_sample_call(conv, state_conv[i].transpose(1, 0, 2), z, ab, gdn_conv_w[i],
                                     gdn_a_log[i], gdn_dt_bias[i], gdn_norm_g[i], state_gdn, i)
        yb = _moba_sample_call(heads(q), heads(k), heads(v), cache_k, cache_v, i, page_table, rel_bias)
        xs = _token_tail(ya, yb.reshape(m_s, MOBA_W), gate, xs, ps, i, lw, tm_s)
        outs['ks'].append(k.reshape(n_s, 1, MOBA_HEADS, MOBA_HD))
        outs['vs'].append(v.reshape(n_s, 1, MOBA_HEADS, MOBA_HD))
        outs['ss'].append(s_new)
        outs['cs'].append(jnp.concatenate([state_conv[i][:, 1:, :], conv[:, None, :]], axis=1))

    st = lambda name: jnp.stack(outs[name])
    return (xp.reshape(n_p, t_p, d), xs.reshape(n_s, 1, d), st('kp'), st('vp'), st('sp'), st('cp'),
            st('ks'), st('vs'), st('ss'), st('cs'))
```

```python
import functools
import math

import numpy as np
import jax
import jax.numpy as jnp
from jax import lax
from jax.experimental import pallas as pl
from jax.experimental.pallas import tpu as pltpu

F32 = jnp.float32
BF16 = jnp.bfloat16
I32 = jnp.int32
HIGHEST = lax.Precision.HIGHEST

D_MODEL = 1024
DEPTH = 4
PAST_LEN = 2048
PAGE_SIZE = 128
GDN_HEADS = 4
GDN_DK = 128
GDN_DV = 128
GDN_CONV = 4
GDN_CHUNK = 64
MOBA_HEADS = 8
MOBA_HD = 64
MOBA_BLOCK = 256
MOBA_TOPK = 3
REL_BUCKETS = 32
REL_MAX_DIST = 128
N_EXPERTS = 32
N_GROUPS = 4
EXPERTS_PER_GROUP = N_EXPERTS // N_GROUPS
TOP_K = 2
D_FF_EXPERT = 512
PLE_DIM = 256
DN_ALPHA = (2 * DEPTH) ** 0.25
LN_EPS = 1e-5
NORM_EPS = 1e-6

GDN_QK = GDN_HEADS * GDN_DK
GDN_VW = GDN_HEADS * GDN_DV
GDN_CONV_CH = 2 * GDN_QK + GDN_VW
MOBA_W = MOBA_HEADS * MOBA_HD
COL_Z = GDN_CONV_CH
COL_A = COL_Z + GDN_VW
COL_B = COL_A + GDN_HEADS
COL_MOBA = COL_B + GDN_HEADS
COL_GATE = COL_MOBA + 3 * MOBA_W
N_IN = COL_GATE + 2 * D_MODEL

LANES = 128
SUBLANES = 8
VMEM_LIMIT_BYTES = 56 * 1024 * 1024

EXPERT_ROWS = 256
GDN_GROUP = 256
NEG_BIG = -1e30


def _cparams(*sem):
    return pltpu.CompilerParams(dimension_semantics=sem, vmem_limit_bytes=VMEM_LIMIT_BYTES)


def _row_tile(m, target):
    t = min(m, target)
    assert m % t == 0, (m, t)
    return t


def _t5_thresholds():
    exact = REL_BUCKETS // 2
    n = np.arange(0, 2 * REL_MAX_DIST)
    log_ratio = np.log(np.maximum(n, 1).astype(np.float32) / np.float32(exact)) / np.float32(
        math.log(REL_MAX_DIST / exact))
    large = exact + (log_ratio * np.float32(REL_BUCKETS - exact)).astype(np.int32)
    bucket = np.where(n < exact, n, np.minimum(large, REL_BUCKETS - 1))
    assert np.all(np.diff(bucket) >= 0)
    return [int(np.argmax(bucket >= k)) for k in range(REL_BUCKETS)]


T5_THRESHOLDS = _t5_thresholds()


def _layer_norm(x, g, b):
    mu = jnp.mean(x, axis=-1, keepdims=True)
    xc = x - mu
    var = jnp.mean(xc * xc, axis=-1, keepdims=True)
    return xc * lax.rsqrt(var + LN_EPS) * g + b


def _sigmoid(x):
    return 1.0 / (1.0 + jnp.exp(-x))


def _silu(x):
    return x * _sigmoid(x)


def _softplus(x):
    return jnp.maximum(x, 0.0) + jnp.log(1.0 + jnp.exp(-jnp.abs(x)))


def _dot(a, b):
    return jnp.dot(a.astype(BF16), b.astype(BF16), preferred_element_type=F32)


def _dot_nt(a, b, precision=None):
    if precision is None:
        a, b = a.astype(BF16), b.astype(BF16)
    return lax.dot_general(a, b, (((1,), (1,)), ((), ())), precision=precision,
                           preferred_element_type=F32)


def _dot_tn(a, b, precision=None):
    if precision is None:
        a, b = a.astype(BF16), b.astype(BF16)
    return lax.dot_general(a, b, (((0,), (0,)), ((), ())), precision=precision,
                           preferred_element_type=F32)


def _ln_kernel(x_ref, g_ref, b_ref, o_ref):
    o_ref[...] = _layer_norm(x_ref[...], g_ref[...], b_ref[...])


def _ln_call(x, g, b):
    m, d = x.shape
    tm = _row_tile(m, 512)
    return pl.pallas_call(
        _ln_kernel,
        out_shape=jax.ShapeDtypeStruct((m, d), F32),
        grid=(m // tm,),
        in_specs=[pl.BlockSpec((tm, d), lambda i: (i, 0)),
                  pl.BlockSpec((1, d), lambda i: (0, 0)),
                  pl.BlockSpec((1, d), lambda i: (0, 0))],
        out_specs=pl.BlockSpec((tm, d), lambda i: (i, 0)),
        compiler_params=_cparams("parallel"),
        name="ln0",
    )(x, g.reshape(1, d), b.reshape(1, d))


N_ROW_PARTS = 5


def _proj_kernel(x_ref, wc_ref, wz_ref, wab_ref, wq_ref, wg_ref, wkt_ref, wvt_ref,
                 conv_ref, z_ref, ab_ref, q_ref, gate_ref, kt_ref, vt_ref):
    x = x_ref[...].astype(BF16)
    for w_ref, o_ref in ((wc_ref, conv_ref), (wz_ref, z_ref), (wab_ref, ab_ref), (wq_ref, q_ref),
                         (wg_ref, gate_ref)):
        o_ref[...] = jnp.dot(x, w_ref[...], preferred_element_type=F32)
    kt_ref[0] = _dot_nt(wkt_ref[...], x)
    vt_ref[0] = _dot_nt(wvt_ref[...], x)


def _split_w_in(w):
    wab = jnp.pad(w[:, COL_A:COL_MOBA], ((0, 0), (0, LANES - 2 * GDN_HEADS)))
    parts = (w[:, :COL_Z], w[:, COL_Z:COL_A], wab, w[:, COL_MOBA:COL_MOBA + MOBA_W], w[:, COL_GATE:],
             w[:, COL_MOBA + MOBA_W:COL_MOBA + 2 * MOBA_W].T, w[:, COL_MOBA + 2 * MOBA_W:COL_GATE].T)
    return tuple(p.astype(BF16) for p in parts)


def _proj_call(x, w_parts, n_seq):
    m, d = x.shape
    t_len = m // n_seq
    tm = _row_tile(t_len, 256)
    tiles = t_len // tm
    row_w = [p.shape[1] for p in w_parts[:N_ROW_PARTS]]
    col_w = [p.shape[0] for p in w_parts[N_ROW_PARTS:]]
    return pl.pallas_call(
        _proj_kernel,
        out_shape=tuple(jax.ShapeDtypeStruct((m, n), F32) for n in row_w)
        + tuple(jax.ShapeDtypeStruct((n_seq, n, t_len), F32) for n in col_w),
        grid=(m // tm,),
        in_specs=[pl.BlockSpec((tm, d), lambda i: (i, 0))]
        + [pl.BlockSpec(p.shape, lambda i: (0, 0)) for p in w_parts],
        out_specs=tuple(pl.BlockSpec((tm, n), lambda i: (i, 0)) for n in row_w)
        + tuple(pl.BlockSpec((1, n, tm), lambda i: (i // tiles, 0, i % tiles)) for n in col_w),
        compiler_params=_cparams("parallel"),
        name="proj",
    )(x, *w_parts)


def _chunk_cumsum(x, pos_in_chunk, axis):
    s = 1
    while s < GDN_CHUNK:
        x = x + jnp.where(pos_in_chunk >= s, pltpu.roll(x, s, axis), 0.0)
        s *= 2
    return x


def _l2_normalize(x):
    return x * lax.rsqrt(jnp.sum(x * x, axis=-1, keepdims=True) + NORM_EPS)


def _gdn_gates(ab, alog_row, dtb_row):
    g = -jnp.exp(alog_row) * _softplus(ab + dtb_row)
    return g, _sigmoid(ab)


GDN_GROUPS_PER_ITER = 4


def _gdn_prompt_kernel(conv_ref, z_ref, ab_ref, w_ref, alog_ref, dtb_ref, ng_ref, y_ref, s_ref,
                       qn_s, kn_s, kb_s, vb_s, gcb_s, gcr_s, u_s, w_s, qg_s, kg_s, intra_s, egl_s):
    t_len = conv_ref.shape[1]
    n_groups = t_len // GDN_GROUP
    n_chunks = t_len // GDN_CHUNK
    cpg = GDN_GROUP // GDN_CHUNK
    groups_per_iter = math.gcd(n_groups, GDN_GROUPS_PER_ITER)

    row = lax.broadcasted_iota(I32, (t_len, LANES), 0)
    lane = lax.broadcasted_iota(I32, (t_len, LANES), 1)

    ab_t = ab_ref[0].T[0:SUBLANES]
    g_rows = -jnp.exp(alog_ref[:, 0:1]) * _softplus(ab_t + dtb_ref[:, 0:1])
    lane_t = lax.broadcasted_iota(I32, (SUBLANES, t_len), 1)
    sub_t = lax.broadcasted_iota(I32, (SUBLANES, t_len), 0)
    gc_rows = _chunk_cumsum(g_rows, lane_t % GDN_CHUNK, 1)
    packed = jnp.where(sub_t < GDN_HEADS, gc_rows, _sigmoid(ab_t))
    cols_all = jnp.concatenate([packed, jnp.zeros((LANES - SUBLANES, t_len), F32)], axis=0).T

    ii = lax.broadcasted_iota(I32, (GDN_GROUP, GDN_GROUP), 0)
    jj = lax.broadcasted_iota(I32, (GDN_GROUP, GDN_GROUP), 1)
    same = (ii // GDN_CHUNK) == (jj // GDN_CHUNK)
    incl = same & (ii >= jj)
    strict = same & (ii > jj)
    eye = (ii == jj).astype(F32)

    def conv(col0, h):
        cols = pl.ds(pl.multiple_of(col0 + h * LANES, LANES), LANES)
        x = conv_ref[0, :, cols]
        w = w_ref[:, cols]
        taps = [w[GDN_CONV - 1 - j:GDN_CONV - j] for j in range(GDN_CONV)]
        rolled = [pltpu.roll(x, j, 0) for j in range(1, GDN_CONV)]
        y = x * taps[0]
        for j in range(1, GDN_CONV):
            y = y + rolled[j - 1] * taps[j]
        top = x[:SUBLANES] * taps[0]
        for j in range(1, GDN_CONV):
            top = top + jnp.where(row[:SUBLANES] >= j, rolled[j - 1][:SUBLANES], 0.0) * taps[j]
        return _silu(jnp.concatenate([top, y[SUBLANES:]], axis=0))

    def head_body(h, carry):
        qn = _l2_normalize(conv(0, h)) * (GDN_DK ** -0.5)
        kn = _l2_normalize(conv(GDN_QK, h))
        vv = conv(2 * GDN_QK, h)
        gc_col = jnp.sum(jnp.where(lane == h, cols_all, 0.0), axis=1, keepdims=True)
        beta_col = jnp.sum(jnp.where(lane == h + GDN_HEADS, cols_all, 0.0), axis=1, keepdims=True)
        gc_row = jnp.sum(jnp.where(sub_t == h, gc_rows, 0.0), axis=0, keepdims=True)
        qn_s[...] = qn
        kn_s[...] = kn
        kb_s[...] = kn * beta_col
        vb_s[...] = vv * beta_col
        gcb_s[...] = jnp.broadcast_to(gc_col, (t_len, LANES))
        for g in range(n_groups):
            gcr_s[g] = jnp.broadcast_to(gc_row[:, g * GDN_GROUP:(g + 1) * GDN_GROUP], (SUBLANES, GDN_GROUP))

        def group_iter(it, c2):
            gs = [it * groups_per_iter + gg for gg in range(groups_per_iter)]
            r0s = [pl.multiple_of(g * GDN_GROUP, GDN_GROUP) for g in gs]
            rows = [pl.ds(r0, GDN_GROUP) for r0 in r0s]
            q_g = [qn_s[r, :] for r in rows]
            k_g = [kn_s[r, :] for r in rows]
            kb_g = [kb_s[r, :] for r in rows]
            vb_g = [vb_s[r, :] for r in rows]
            gcb_g = [gcb_s[r, :] for r in rows]
            gcol = [x[:, 0:1] for x in gcb_g]
            eg = [jnp.exp(x) for x in gcol]
            decay = [jnp.where(incl, jnp.exp(jnp.where(incl, gc - gcr_s[g][0:1, :], 0.0)), 0.0)
                     for gc, g in zip(gcol, gs)]
            kk = [_dot_nt(a, b) for a, b in zip(kb_g, k_g)]
            qk = [_dot_nt(a, b) for a, b in zip(q_g, k_g)]
            a_mat = [jnp.where(strict, x * d, 0.0) for x, d in zip(kk, decay)]
            intra = [x * d for x, d in zip(qk, decay)]
            t_inv = [eye - a for a in a_mat]
            x_pow = a_mat
            p = 2
            while p < GDN_CHUNK:
                x_pow = [_dot(x, x) for x in x_pow]
                t_inv = [t + _dot(t, x) for t, x in zip(t_inv, x_pow)]
                p *= 2
            uw = [_dot(t, jnp.concatenate([vb, kb * e], axis=1)) for t, vb, kb, e in zip(t_inv, vb_g, kb_g, eg)]
            for i, g in enumerate(gs):
                u_s[h, rows[i], :] = uw[i][:, :GDN_DV]
                w_s[h, rows[i], :] = uw[i][:, GDN_DV:].astype(w_s.dtype)
                qg_s[h, rows[i], :] = (q_g[i] * eg[i]).astype(qg_s.dtype)
                for c in range(cpg):
                    lo = c * GDN_CHUNK
                    g_last = gcb_g[i][lo + GDN_CHUNK - 1:lo + GDN_CHUNK, :]
                    rows_c = pl.ds(pl.multiple_of(r0s[i] + lo, GDN_CHUNK), GDN_CHUNK)
                    kg = k_g[i][lo:lo + GDN_CHUNK] * jnp.exp(g_last - gcb_g[i][lo:lo + GDN_CHUNK])
                    kg_s[h, rows_c, :] = kg.astype(kg_s.dtype)
                    intra_s[h, rows_c, :] = intra[i][lo:lo + GDN_CHUNK, lo:lo + GDN_CHUNK].astype(intra_s.dtype)
                    egl_s[h, g * cpg + c] = jnp.broadcast_to(jnp.exp(g_last), (SUBLANES, LANES))
            return c2

        lax.fori_loop(0, n_groups // groups_per_iter, group_iter, 0)
        return carry

    lax.fori_loop(0, GDN_HEADS, head_body, 0)

    ng = ng_ref[...]

    def chunk_body(c, s_mats):
        r0 = pl.multiple_of(c * GDN_CHUNK, GDN_CHUNK)
        rows = pl.ds(r0, GDN_CHUNK)
        heads = range(GDN_HEADS)
        wq = [jnp.concatenate([w_s[h, rows, :], qg_s[h, rows, :]], axis=0) for h in heads]
        ws_qs = [_dot(wq[h], s_mats[h]) for h in heads]
        v_new = [u_s[h, rows, :] - ws_qs[h][:GDN_CHUNK] for h in heads]
        s_out = [s_mats[h] * egl_s[h, c][0:1, :] + _dot_tn(kg_s[h, rows, :], v_new[h]) for h in heads]
        o_all = [ws_qs[h][GDN_CHUNK:] + _dot(intra_s[h, rows, :], v_new[h]) for h in heads]
        for h in heads:
            cols = slice(h * GDN_DV, (h + 1) * GDN_DV)
            o = o_all[h]
            o = o * lax.rsqrt(jnp.mean(o * o, axis=-1, keepdims=True) + NORM_EPS) * ng * _silu(z_ref[0, rows, cols])
            y_ref[0, rows, cols] = o.astype(y_ref.dtype)
        return tuple(s_out)

    zero = jnp.zeros((GDN_DK, GDN_DV), F32)
    s_fin = lax.fori_loop(0, n_chunks, chunk_body, (zero,) * GDN_HEADS)
    for h in range(GDN_HEADS):
        s_ref[0, h] = s_fin[h]


def _gdn_prompt_call(conv, z, ab, conv_w, a_log, dt_bias, norm_g):
    n, t_len, c = conv.shape
    tile = lambda v: jnp.broadcast_to(jnp.pad(v, (0, SUBLANES - GDN_HEADS))[:, None], (SUBLANES, LANES))
    once = pl.Buffered(1)
    small = lambda shape: pl.BlockSpec(shape, lambda i: (0, 0))
    per_head = lambda w, dt: pltpu.VMEM((GDN_HEADS, t_len, w), dt)
    big = pltpu.VMEM((t_len, LANES), F32)
    return pl.pallas_call(
        _gdn_prompt_kernel,
        out_shape=(jax.ShapeDtypeStruct((n, t_len, GDN_VW), BF16),
                   jax.ShapeDtypeStruct((n, GDN_HEADS, GDN_DK, GDN_DV), F32)),
        grid=(n,),
        in_specs=[pl.BlockSpec((1, t_len, c), lambda i: (i, 0, 0), pipeline_mode=once),
                  pl.BlockSpec((1, t_len, GDN_VW), lambda i: (i, 0, 0), pipeline_mode=once),
                  pl.BlockSpec((1, t_len, LANES), lambda i: (i, 0, 0)),
                  small((GDN_CONV, c)), small((SUBLANES, LANES)), small((SUBLANES, LANES)),
                  small((1, GDN_DV))],
        out_specs=(pl.BlockSpec((1, t_len, GDN_VW), lambda i: (i, 0, 0)),
                   pl.BlockSpec((1, GDN_HEADS, GDN_DK, GDN_DV), lambda i: (i, 0, 0, 0))),
        scratch_shapes=[big, big, big, big, big,
                        pltpu.VMEM((t_len // GDN_GROUP, SUBLANES, GDN_GROUP), F32),
                        per_head(GDN_DV, F32), per_head(GDN_DK, BF16), per_head(GDN_DK, BF16),
                        per_head(GDN_DK, BF16), per_head(GDN_CHUNK, BF16),
                        pltpu.VMEM((GDN_HEADS, t_len // GDN_CHUNK, SUBLANES, LANES), F32)],
        compiler_params=_cparams("parallel"),
        name="gdn_prompt",
    )(conv, z, ab, conv_w, tile(a_log), tile(dt_bias), norm_g.reshape(1, GDN_DV))


HEADS_PER_TILE = LANES // MOBA_HD


def _t5_bias(dist, rb_ref, head):
    b = jnp.full(dist.shape, rb_ref[head, 0], F32)
    for k in range(1, REL_BUCKETS):
        b = jnp.where(dist >= T5_THRESHOLDS[k], rb_ref[head, k], b)
    return b


def _top_blocks(gate_t, n_valid):
    nq = gate_t.shape[1]
    sub = lax.broadcasted_iota(I32, gate_t.shape, 0)
    g = jnp.where(sub < n_valid, gate_t, -jnp.inf)
    rank = jnp.zeros(g.shape, F32)
    for b2 in range(SUBLANES):
        row = g[b2:b2 + 1, :]
        rank = rank + jnp.where(row > g, 1.0, 0.0) + jnp.where(row == g, jnp.where(sub > b2, 1.0, 0.0), 0.0)
    sel_t = jnp.where((rank < MOBA_TOPK) & (sub < n_valid), 1.0, 0.0)
    return jnp.concatenate([sel_t, jnp.zeros((LANES - SUBLANES, nq), F32)], axis=0).T


def _moba_prompt_kernel(rb_ref, q_ref, kt_ref, vt_ref, o_ref, kmean_s, biasd_s, biasp_s, logit_s, m_s,
                        acc_s):
    hp = pl.program_id(0)
    n = pl.program_id(1)
    qb = pl.program_id(2)
    blk = MOBA_BLOCK
    n_blocks = kt_ref.shape[2] // blk
    half = blk // 2
    ii = lax.broadcasted_iota(I32, (blk, blk), 0)
    jj = lax.broadcasted_iota(I32, (blk, blk), 1)
    lane = lax.broadcasted_iota(I32, (blk, LANES), 1)

    @pl.when((n == 0) & (qb == 0))
    def _():
        for e in range(HEADS_PER_TILE):
            head = hp * HEADS_PER_TILE + e
            biasd_s[e] = jnp.where(ii >= jj, _t5_bias(ii - jj, rb_ref, head), NEG_BIG)
            biasp_s[e] = _t5_bias(blk + ii - jj, rb_ref, head)

    @pl.when(qb == 0)
    def _():
        cols = [jnp.mean(kt_ref[0, :, b * blk:(b + 1) * blk], axis=1, keepdims=True) for b in range(n_blocks)]
        cols.append(jnp.zeros((LANES, LANES - n_blocks), F32))
        kmean_s[...] = jnp.concatenate(cols, axis=1).T

    q2 = q_ref[0]
    own_cols = pl.ds(pl.multiple_of(qb * blk, blk), blk)
    kt_own = kt_ref[0, :, own_cols].astype(BF16)
    scale = MOBA_HD ** -0.5

    q_scaled, sels, fars, m_init = [], [], [], []
    for e in range(HEADS_PER_TILE):
        qh = jnp.where(lane // MOBA_HD == e, q2, 0.0)
        gate_t = _dot_nt(kmean_s[0:SUBLANES, :], qh, precision=HIGHEST)
        sels.append(_top_blocks(gate_t, qb))
        fars.append(rb_ref[hp * HEADS_PER_TILE + e, REL_BUCKETS - 1])
        qs = (qh * scale).astype(BF16)
        q_scaled.append(qs)
        s = jnp.dot(qs, kt_own, preferred_element_type=F32) + biasd_s[e]
        logit_s[e, :, own_cols] = s
        m_init.append(jnp.maximum(s[:, :half], s[:, half:]))
    heads = range(HEADS_PER_TILE)

    def masked_logits(bs, tile_bias):
        cols = [pl.ds(pl.multiple_of(b * blk, blk), blk) for b in bs]
        kt_b = [kt_ref[0, :, c].astype(BF16) for c in cols]
        raw = [[jnp.dot(q_scaled[e], k, preferred_element_type=F32) for e in heads] for k in kt_b]
        picked = [[jnp.sum(jnp.where(lane == b, sels[e], 0.0), axis=-1, keepdims=True) for e in heads]
                  for b in bs]
        m_out = [None] * HEADS_PER_TILE
        for i in range(len(bs)):
            for e in heads:
                if tile_bias[e].ndim == 0:
                    s = raw[i][e] + jnp.where(picked[i][e] > 0.0, tile_bias[e], NEG_BIG)
                else:
                    s = raw[i][e] + tile_bias[e] + jnp.where(picked[i][e] > 0.0, 0.0, NEG_BIG)
                logit_s[e, :, cols[i]] = s
                m_blk = jnp.maximum(s[:, :half], s[:, half:])
                m_out[e] = m_blk if m_out[e] is None else jnp.maximum(m_out[e], m_blk)
        return m_out

    n_far = jnp.maximum(qb - 1, 0)

    def far_pair(it, m_parts):
        m_blk = masked_logits([2 * it, 2 * it + 1], fars)
        return tuple(jnp.maximum(m_parts[e], m_blk[e]) for e in heads)

    m_parts = lax.fori_loop(0, n_far // 2, far_pair, tuple(m_init))
    for e in heads:
        m_s[e] = m_parts[e]

    @pl.when(n_far % 2 == 1)
    def _():
        m_blk = masked_logits([n_far - 1], fars)
        for e in heads:
            m_s[e] = jnp.maximum(m_s[e], m_blk[e])

    @pl.when(qb >= 1)
    def _():
        m_blk = masked_logits([qb - 1], [biasp_s[e] for e in heads])
        for e in heads:
            m_s[e] = jnp.maximum(m_s[e], m_blk[e])

    m_rows = [jnp.max(m_s[e], axis=-1, keepdims=True) for e in heads]

    row = lax.broadcasted_iota(I32, (LANES, blk), 0)
    ones_row = [((e + 1) % HEADS_PER_TILE) * MOBA_HD for e in heads]

    def weigh_blocks(bs, accs):
        cols = [pl.ds(pl.multiple_of(b * blk, blk), blk) for b in bs]
        vt_b = [vt_ref[0, :, c] for c in cols]
        p = [[jnp.exp(logit_s[e, :, c] - m_rows[e]) for e in heads] for c in cols]
        vt_aug = [[jnp.where(row // MOBA_HD == e, v, jnp.where(row == ones_row[e], 1.0, 0.0)) for e in heads]
                  for v in vt_b]
        pv = [[_dot_nt(p[i][e], vt_aug[i][e]) for e in heads] for i in range(len(bs))]
        out = list(accs)
        for i in range(len(bs)):
            for e in heads:
                out[e] = out[e] + pv[i][e]
        return tuple(out)

    n_vis = qb + 1
    zero = jnp.zeros((blk, LANES), F32)
    accs = lax.fori_loop(0, n_vis // 2, lambda it, a: weigh_blocks([2 * it, 2 * it + 1], a),
                         (zero,) * HEADS_PER_TILE)
    for e in heads:
        acc_s[e] = accs[e]

    @pl.when(n_vis % 2 == 1)
    def _():
        last = weigh_blocks([qb], tuple(acc_s[e] for e in heads))
        for e in heads:
            acc_s[e] = last[e]

    fin = [acc_s[e] for e in heads]
    res = fin[0] / fin[0][:, ones_row[0]:ones_row[0] + 1]
    for e in range(1, HEADS_PER_TILE):
        res = jnp.where(lane // MOBA_HD == e, fin[e] / fin[e][:, ones_row[e]:ones_row[e] + 1], res)
    o_ref[0] = res.astype(o_ref.dtype)


def _moba_prompt_call(q, kt, vt, rel_bias):
    n, t_len, _ = q.shape
    blk = MOBA_BLOCK
    assert blk == 2 * LANES and t_len // blk <= SUBLANES
    n_tiles = MOBA_W // LANES
    kv_spec = pl.BlockSpec((1, LANES, t_len), lambda hp, i, qb, rb: (i, hp, 0))
    q_spec = pl.BlockSpec((1, blk, LANES), lambda hp, i, qb, rb: (i, qb, hp))
    return pl.pallas_call(
        _moba_prompt_kernel,
        out_shape=jax.ShapeDtypeStruct((n, t_len, MOBA_W), BF16),
        grid_spec=pltpu.PrefetchScalarGridSpec(
            num_scalar_prefetch=1,
            grid=(n_tiles, n, t_len // blk),
            in_specs=[q_spec, kv_spec, kv_spec],
            out_specs=q_spec,
            scratch_shapes=[pltpu.VMEM((LANES, LANES), F32),
                            pltpu.VMEM((HEADS_PER_TILE, blk, blk), F32),
                            pltpu.VMEM((HEADS_PER_TILE, blk, blk), F32),
                            pltpu.VMEM((HEADS_PER_TILE, blk, t_len), F32),
                            pltpu.VMEM((HEADS_PER_TILE, blk, LANES), F32),
                            pltpu.VMEM((HEADS_PER_TILE, blk, LANES), F32)]),
        compiler_params=_cparams("arbitrary", "arbitrary", "arbitrary"),
        name="moba_prompt",
    )(rel_bias.T, q, kt, vt)


MERGE_SUB_ROWS = 256


def _merge_kernel(ya_ref, yb_ref, gate_ref, x_ref, p_ref, wug_ref, wum_ref, wo_ref, wpg_ref, wpp_ref,
                  wrt_ref, g_ref, b_ref, h_ref, ple_ref, aff_ref):
    tm = x_ref.shape[0]
    sub = min(tm, MERGE_SUB_ROWS)
    parts = [slice(k * sub, (k + 1) * sub) for k in range(tm // sub)]
    up_a = [jnp.dot(ya_ref[r, :], wug_ref[...], preferred_element_type=F32) for r in parts]
    up_b = [jnp.dot(yb_ref[r, :], wum_ref[...], preferred_element_type=F32) for r in parts]
    merged = [_sigmoid(gate_ref[r, :D_MODEL]) * a + _sigmoid(gate_ref[r, D_MODEL:]) * b
              for r, a, b in zip(parts, up_a, up_b)]
    proj = [_dot(mg, wo_ref[...]) for mg in merged]
    h = [_layer_norm(DN_ALPHA * x_ref[r, :] + o, g_ref[...], b_ref[...]) for r, o in zip(parts, proj)]
    for r, hh in zip(parts, h):
        h_ref[r, :] = hh
    ple_gate = [_dot(hh, wpg_ref[...]) for hh in h]
    ple_val = [_dot(p_ref[0, r, :], wpp_ref[...]) for r in parts]
    for r, pg, pv in zip(parts, ple_gate, ple_val):
        ple_ref[r, :] = _sigmoid(pg) * pv
    for r, hh in zip(parts, h):
        aff_ref[:, r] = _sigmoid(_dot_nt(wrt_ref[...], hh, precision=HIGHEST))


def _merge_call(ya, yb, gate, x, p, layer, wug, wum, wo, wpg, wpp, wrt, ln_g, ln_b, tm):
    m, d = x.shape
    rows = lambda n: pl.BlockSpec((tm, n), lambda i: (i, 0))
    full = lambda a: pl.BlockSpec(a.shape, lambda i: (0, 0))
    return pl.pallas_call(
        _merge_kernel,
        out_shape=(jax.ShapeDtypeStruct((m, d), F32), jax.ShapeDtypeStruct((m, d), F32),
                   jax.ShapeDtypeStruct((N_EXPERTS, m), F32)),
        grid=(m // tm,),
        in_specs=[rows(GDN_VW), rows(MOBA_W), rows(2 * d), rows(d),
                  pl.BlockSpec((1, tm, PLE_DIM), lambda i: (layer, i, 0)),
                  full(wug), full(wum), full(wo), full(wpg), full(wpp), full(wrt),
                  pl.BlockSpec((1, d), lambda i: (0, 0)), pl.BlockSpec((1, d), lambda i: (0, 0))],
        out_specs=(rows(d), rows(d), pl.BlockSpec((N_EXPERTS, tm), lambda i: (0, i))),
        compiler_params=_cparams("parallel"),
        name="merge",
    )(ya, yb, gate, x, p, wug, wum, wo, wpg, wpp, wrt, ln_g.reshape(1, d), ln_b.reshape(1, d))


def _route_kernel(aff_ref, br_ref, eid_ref, wt_ref, pos_ref, cnt_ref, carry_s, tri_s):
    i = pl.program_id(0)
    tn = aff_ref.shape[1]

    @pl.when(i == 0)
    def _():
        carry_s[...] = jnp.zeros(carry_s.shape, F32)
        r = lax.broadcasted_iota(I32, (tn, tn), 0)
        c = lax.broadcasted_iota(I32, (tn, tn), 1)
        tri_s[...] = jnp.where(r < c, 1.0, 0.0).astype(tri_s.dtype)

    aff = aff_ref[...]
    sel = aff + br_ref[...]
    gsz = EXPERTS_PER_GROUP
    sub = lax.broadcasted_iota(I32, (gsz, tn), 0)
    best = None
    for g in range(N_GROUPS):
        v = sel[g * gsz:(g + 1) * gsz]
        a = aff[g * gsz:(g + 1) * gsz]
        m1 = jnp.max(v, axis=0, keepdims=True)
        i1 = jnp.min(jnp.where(v == m1, sub, gsz), axis=0, keepdims=True)
        v2 = jnp.where(sub == i1, -jnp.inf, v)
        m2 = jnp.max(v2, axis=0, keepdims=True)
        i2 = jnp.min(jnp.where(v2 == m2, sub, gsz), axis=0, keepdims=True)
        a1 = jnp.sum(jnp.where(sub == i1, a, 0.0), axis=0, keepdims=True)
        a2 = jnp.sum(jnp.where(sub == i2, a, 0.0), axis=0, keepdims=True)
        cand = (m1 + m2, i1 + g * gsz, i2 + g * gsz, a1, a2)
        if best is None:
            best = cand
        else:
            better = cand[0] > best[0]
            best = tuple(jnp.where(better, c, b) for c, b in zip(cand, best))
    _, e1, e2, a1, a2 = best
    eid_ref[0, 0:1, :] = e1
    eid_ref[0, 1:2, :] = e2
    wt_ref[0, 0:1, :] = a1 / (a1 + a2)
    wt_ref[0, 1:2, :] = a2 / (a1 + a2)

    e_iota = lax.broadcasted_iota(I32, (N_EXPERTS, tn), 0)
    oh1 = jnp.where(e_iota == e1, 1.0, 0.0)
    oh2 = jnp.where(e_iota == e2, 1.0, 0.0)
    tri = tri_s[...]
    tot1 = jnp.sum(oh1, axis=1, keepdims=True)
    tot2 = jnp.sum(oh2, axis=1, keepdims=True)
    base = carry_s[:, 0:1]
    c1 = base + jnp.dot(oh1.astype(tri.dtype), tri, preferred_element_type=F32)
    c2 = base + tot1 + jnp.dot(oh2.astype(tri.dtype), tri, preferred_element_type=F32)
    pos_ref[0, 0:1, :] = jnp.sum(oh1 * c1, axis=0, keepdims=True).astype(I32)
    pos_ref[0, 1:2, :] = jnp.sum(oh2 * c2, axis=0, keepdims=True).astype(I32)
    carry_s[...] = carry_s[...] + (tot1 + tot2)
    cnt_ref[...] = carry_s[...]


def _route_call(aff_t, b_router, tn):
    m = aff_t.shape[1]
    nt = m // tn
    tok = lambda dt: jax.ShapeDtypeStruct((nt, TOP_K, tn), dt)
    tok_spec = pl.BlockSpec((1, TOP_K, tn), lambda i: (i, 0, 0))
    return pl.pallas_call(
        _route_kernel,
        out_shape=(tok(I32), tok(F32), tok(I32), jax.ShapeDtypeStruct((N_EXPERTS, LANES), F32)),
        grid=(nt,),
        in_specs=[pl.BlockSpec((N_EXPERTS, tn), lambda i: (0, i)),
                  pl.BlockSpec((N_EXPERTS, 1), lambda i: (0, 0))],
        out_specs=(tok_spec, tok_spec, tok_spec, pl.BlockSpec((N_EXPERTS, LANES), lambda i: (0, 0))),
        scratch_shapes=[pltpu.VMEM((N_EXPERTS, LANES), F32), pltpu.VMEM((tn, tn), BF16)],
        compiler_params=_cparams("arbitrary"),
        name="route",
    )(aff_t, b_router.reshape(N_EXPERTS, 1))


ROW_DMA_UNROLL = 8


def _dest_kernel(pstart_ref, eid_ref, pos_ref, dest_ref):
    eid = eid_ref[0]
    dest = pos_ref[0]
    for e in range(N_EXPERTS):
        dest = dest + jnp.where(eid == e, pstart_ref[e], 0)
    dest_ref[0] = dest


def _dest_call(pstart, eid, pos):
    nt, _, tm = eid.shape
    tok = pl.BlockSpec((1, TOP_K, tm), lambda i, ps: (i, 0, 0))
    return pl.pallas_call(
        _dest_kernel,
        out_shape=jax.ShapeDtypeStruct(eid.shape, I32),
        grid_spec=pltpu.PrefetchScalarGridSpec(num_scalar_prefetch=1, grid=(nt,), in_specs=[tok, tok],
                                               out_specs=tok),
        compiler_params=_cparams("parallel"),
        name="dest",
    )(pstart, eid, pos)


def _dispatch_kernel(dest_ref, h_ref, xs_in_ref, xs_ref, sem):
    del xs_in_ref
    tm = h_ref.shape[0]

    def issue(it, c):
        for u in range(ROW_DMA_UNROLL):
            t = it * ROW_DMA_UNROLL + u
            for s in range(TOP_K):
                pltpu.make_async_copy(h_ref.at[pl.ds(t, 1)], xs_ref.at[pl.ds(dest_ref[0, s, t], 1)], sem).start()
        return c

    lax.fori_loop(0, tm // ROW_DMA_UNROLL, issue, 0)
    for s in range(TOP_K):
        pltpu.make_async_copy(h_ref, xs_ref.at[pl.ds(0, tm)], sem).wait()


def _dispatch_call(dest, h, xs_init):
    m, d = h.shape
    nt, _, tm = dest.shape
    n_rows = xs_init.shape[0]
    assert tm % ROW_DMA_UNROLL == 0 and nt * tm == m
    return pl.pallas_call(
        _dispatch_kernel,
        out_shape=jax.ShapeDtypeStruct((n_rows, d), F32),
        grid=(nt,),
        in_specs=[pl.BlockSpec((1, TOP_K, tm), lambda i: (i, 0, 0), memory_space=pltpu.SMEM),
                  pl.BlockSpec((tm, d), lambda i: (i, 0)),
                  pl.BlockSpec(memory_space=pl.ANY)],
        out_specs=pl.BlockSpec(memory_space=pl.ANY),
        scratch_shapes=[pltpu.SemaphoreType.DMA(())],
        input_output_aliases={2: 0},
        compiler_params=_cparams("arbitrary"),
        name="dispatch",
    )(dest, h, xs_init)


EXPERT_FF_SPLIT = 2


def _expert_kernel(be_ref, nused_ref, xs_ref, wg_ref, wu_ref, wd_ref, ys_ref, wg_s, wu_s, wd_s):
    j = pl.program_id(0)

    @pl.when(j < nused_ref[0])
    def _():
        @pl.when((j == 0) | (be_ref[j] != be_ref[jnp.maximum(j - 1, 0)]))
        def _():
            wg_s[...] = wg_ref[0, 0].astype(wg_s.dtype)
            wu_s[...] = wu_ref[0, 0].astype(wu_s.dtype)
            wd_s[...] = wd_ref[0, 0].astype(wd_s.dtype)

        x = xs_ref[...].astype(BF16)
        fw = wg_s.shape[1] // EXPERT_FF_SPLIT
        cuts = [slice(k * fw, (k + 1) * fw) for k in range(EXPERT_FF_SPLIT)]
        gate = [_dot(x, wg_s[:, c]) for c in cuts]
        up = [_dot(x, wu_s[:, c]) for c in cuts]
        hid = [_silu(g) * u for g, u in zip(gate, up)]
        down = [_dot(hh, wd_s[c, :]) for hh, c in zip(hid, cuts)]
        ys_ref[...] = functools.reduce(lambda a, b: a + b, down)

    @pl.when(j >= nused_ref[0])
    def _():
        ys_ref[...] = jnp.zeros(ys_ref.shape, F32)


def _expert_call(block_e, n_used, xs, w_gate, w_up, w_down, layer):
    n_rows, d = xs.shape
    f = w_gate.shape[3]
    nb = n_rows // EXPERT_ROWS
    return pl.pallas_call(
        _expert_kernel,
        out_shape=jax.ShapeDtypeStruct((n_rows, d), F32),
        grid_spec=pltpu.PrefetchScalarGridSpec(
            num_scalar_prefetch=2,
            grid=(nb,),
            in_specs=[pl.BlockSpec((EXPERT_ROWS, d), lambda j, be, nu: (j, 0)),
                      pl.BlockSpec((1, 1, d, f), lambda j, be, nu: (layer, be[j], 0, 0)),
                      pl.BlockSpec((1, 1, d, f), lambda j, be, nu: (layer, be[j], 0, 0)),
                      pl.BlockSpec((1, 1, f, d), lambda j, be, nu: (layer, be[j], 0, 0))],
            out_specs=pl.BlockSpec((EXPERT_ROWS, d), lambda j, be, nu: (j, 0)),
            scratch_shapes=[pltpu.VMEM((d, f), BF16), pltpu.VMEM((d, f), BF16), pltpu.VMEM((f, d), BF16)]),
        compiler_params=_cparams("arbitrary"),
        name="experts",
    )(block_e, n_used, xs, w_gate, w_up, w_down)


def _combine_kernel(dest_ref, dest_next_ref, wt_ref, h_ref, ple_ref, ys_ref, g_ref, b_ref, o_ref, buf, sem):
    i = pl.program_id(0)
    nt = pl.num_programs(0)
    tm = h_ref.shape[0]
    half = i % 2

    def issue_tile(idx_ref, slot):
        def issue(it, c):
            for u in range(ROW_DMA_UNROLL):
                t = it * ROW_DMA_UNROLL + u
                for s in range(TOP_K):
                    pltpu.make_async_copy(ys_ref.at[pl.ds(idx_ref[0, s, t], 1)],
                                          buf.at[slot, s, pl.ds(t, 1)], sem.at[slot]).start()
            return c

        lax.fori_loop(0, tm // ROW_DMA_UNROLL, issue, 0)

    @pl.when(i == 0)
    def _():
        issue_tile(dest_ref, 0)

    @pl.when(i + 1 < nt)
    def _():
        issue_tile(dest_next_ref, 1 - half)

    for s in range(TOP_K):
        pltpu.make_async_copy(ys_ref.at[pl.ds(0, tm)], buf.at[half, s], sem.at[half]).wait()

    w_cols = jnp.concatenate([wt_ref[0], jnp.zeros((LANES - TOP_K, tm), F32)], axis=0).T
    ffn = buf[half, 0] * w_cols[:, 0:1] + buf[half, 1] * w_cols[:, 1:2]
    o_ref[...] = _layer_norm(DN_ALPHA * h_ref[...] + ffn + ple_ref[...], g_ref[...], b_ref[...])


def _combine_call(dest, wt, h, ple, ys, ln_g, ln_b):
    m, d = h.shape
    nt, _, tm = dest.shape
    assert tm % ROW_DMA_UNROLL == 0
    tok = lambda **kw: pl.BlockSpec((1, TOP_K, tm), lambda i: (i, 0, 0), **kw)
    tok_next = pl.BlockSpec((1, TOP_K, tm), lambda i: (jnp.minimum(i + 1, nt - 1), 0, 0),
                            memory_space=pltpu.SMEM)
    rows = pl.BlockSpec((tm, d), lambda i: (i, 0))
    vec = pl.BlockSpec((1, d), lambda i: (0, 0))
    return pl.pallas_call(
        _combine_kernel,
        out_shape=jax.ShapeDtypeStruct((m, d), F32),
        grid=(nt,),
        in_specs=[tok(memory_space=pltpu.SMEM), tok_next, tok(), rows, rows,
                  pl.BlockSpec(memory_space=pl.ANY), vec, vec],
        out_specs=rows,
        scratch_shapes=[pltpu.VMEM((2, TOP_K, tm, d), F32), pltpu.SemaphoreType.DMA((2,))],
        compiler_params=_cparams("arbitrary"),
        name="combine",
    )(dest, dest, wt, h, ple, ys, ln_g.reshape(1, d), ln_b.reshape(1, d))


SEQS_PER_STEP = SUBLANES


def _gdn_sample_kernel(new_ref, prev_ref, z_ref, ab_ref, w_ref, alog_ref, dtb_ref, ng_ref, s_ref,
                       y_ref, so_ref):
    w = w_ref[...]
    pre = prev_ref[0] * w[0:1]
    for j in range(1, GDN_CONV - 1):
        pre = pre + prev_ref[j] * w[j:j + 1]
    qkv = _silu(pre + new_ref[...] * w[GDN_CONV - 1:GDN_CONV])
    g_all, beta_all = _gdn_gates(ab_ref[...], alog_ref[...], dtb_ref[...])
    ng = ng_ref[...]
    z = z_ref[...]
    pad = jnp.zeros((LANES - SEQS_PER_STEP, GDN_DK), F32)
    for h in range(GDN_HEADS):
        q = _l2_normalize(qkv[:, h * GDN_DK:(h + 1) * GDN_DK]) * (GDN_DK ** -0.5)
        k = _l2_normalize(qkv[:, GDN_QK + h * GDN_DK:GDN_QK + (h + 1) * GDN_DK])
        v = qkv[:, 2 * GDN_QK + h * GDN_DV:2 * GDN_QK + (h + 1) * GDN_DV]
        eg = jnp.exp(g_all[:, h:h + 1])
        beta = beta_all[:, GDN_HEADS + h:GDN_HEADS + h + 1]
        qk = jnp.sum(q * k, axis=-1, keepdims=True)
        k_t = jnp.concatenate([k, pad], axis=0).T
        qg_t = jnp.concatenate([q * eg, pad], axis=0).T
        outs = []
        for i in range(SEQS_PER_STEP):
            s_mat = s_ref[0, i, h]
            k_col = k_t[:, i:i + 1]
            k_s = jnp.sum(k_col * s_mat, axis=0, keepdims=True)
            q_s = jnp.sum(qg_t[:, i:i + 1] * s_mat, axis=0, keepdims=True)
            v_new = beta[i:i + 1] * (v[i:i + 1] - eg[i:i + 1] * k_s)
            outs.append(q_s + qk[i:i + 1] * v_new)
            so_ref[i, h] = s_mat * eg[i:i + 1] + k_col * v_new
        o = jnp.concatenate(outs, axis=0)
        o = o * lax.rsqrt(jnp.mean(o * o, axis=-1, keepdims=True) + NORM_EPS) * ng
        y_ref[:, h * GDN_DV:(h + 1) * GDN_DV] = (o * _silu(z[:, h * GDN_DV:(h + 1) * GDN_DV])).astype(y_ref.dtype)


def _gdn_sample_call(conv_new, conv_prev_t, z, ab, conv_w, a_log, dt_bias, norm_g, state, layer):
    n, c = conv_new.shape
    sp = SEQS_PER_STEP
    alog_row = jnp.pad(a_log, (0, LANES - GDN_HEADS)).reshape(1, LANES)
    dtb_row = jnp.pad(dt_bias, (0, LANES - GDN_HEADS)).reshape(1, LANES)
    row_spec = pl.BlockSpec((1, LANES), lambda i: (0, 0))
    s_spec = pl.BlockSpec((sp, GDN_HEADS, GDN_DK, GDN_DV), lambda i: (i, 0, 0, 0))
    s_in_spec = pl.BlockSpec((1, sp, GDN_HEADS, GDN_DK, GDN_DV), lambda i: (layer, i, 0, 0, 0))
    return pl.pallas_call(
        _gdn_sample_kernel,
        out_shape=(jax.ShapeDtypeStruct((n, GDN_VW), BF16), jax.ShapeDtypeStruct(state.shape[1:], F32)),
        grid=(n // sp,),
        in_specs=[pl.BlockSpec((sp, c), lambda i: (i, 0)),
                  pl.BlockSpec((GDN_CONV - 1, sp, c), lambda i: (0, i, 0)),
                  pl.BlockSpec((sp, GDN_VW), lambda i: (i, 0)),
                  pl.BlockSpec((sp, LANES), lambda i: (i, 0)),
                  pl.BlockSpec((GDN_CONV, c), lambda i: (0, 0)),
                  row_spec, row_spec, row_spec, s_in_spec],
        out_specs=(pl.BlockSpec((sp, GDN_VW), lambda i: (i, 0)), s_spec),
        compiler_params=_cparams("parallel"),
        name="gdn_sample",
    )(conv_new, conv_prev_t, z, ab, conv_w, alog_row, dtb_row, norm_g.reshape(1, GDN_DV), state)


PAGES_PER_SEQ = PAST_LEN // PAGE_SIZE
PAGES_PER_BLOCK = MOBA_BLOCK // PAGE_SIZE
PAST_BLOCKS = PAST_LEN // MOBA_BLOCK


def _moba_sample_kernel(pt_ref, q_ref, qc_ref, kn_ref, vn_ref, rb_ref, *rest):
    del pt_ref
    np_ = PAGES_PER_SEQ
    k_pages = rest[:np_]
    v_pages = rest[np_:2 * np_]
    o_ref, bias_s = rest[2 * np_:]
    n = pl.program_id(0)
    nh, hd = MOBA_HEADS, MOBA_HD
    rb = rb_ref[...]
    lane_pos = lax.broadcasted_iota(I32, (nh, PAST_LEN), 1)

    @pl.when(n == 0)
    def _():
        dist = PAST_LEN - lane_pos
        b = jnp.broadcast_to(rb[:, 0:1], (nh, PAST_LEN))
        for kk in range(1, REL_BUCKETS):
            b = jnp.where(dist >= T5_THRESHOLDS[kk], rb[:, kk:kk + 1], b)
        bias_s[...] = b

    q = q_ref[0]
    q_cols = [jnp.broadcast_to(qc_ref[0, h], (hd, PAGE_SIZE)) for h in range(nh)]

    page_rows = []
    for j in range(np_):
        rows = [jnp.sum(k_pages[j][0, 0, h] * q_cols[h], axis=0, keepdims=True) for h in range(nh)]
        page_rows.append(jnp.concatenate(rows, axis=0))
    s_raw = jnp.concatenate(page_rows, axis=1)

    gates = [jnp.sum(s_raw[:, b * MOBA_BLOCK:(b + 1) * MOBA_BLOCK], axis=1, keepdims=True) * (1.0 / MOBA_BLOCK)
             for b in range(PAST_BLOCKS)]
    sel = jnp.zeros((nh, PAST_LEN), F32)
    for b in range(PAST_BLOCKS):
        rank = jnp.zeros((nh, 1), F32)
        for b2 in range(PAST_BLOCKS):
            if b2 != b:
                ahead = (gates[b2] > gates[b]) | (gates[b2] == gates[b]) if b2 < b else gates[b2] > gates[b]
                rank = rank + jnp.where(ahead, 1.0, 0.0)
        sel = jnp.where(lane_pos // MOBA_BLOCK == b, jnp.where(rank < MOBA_TOPK, 1.0, 0.0), sel)

    scale = hd ** -0.5
    s = jnp.where(sel > 0.0, s_raw * scale + bias_s[...], NEG_BIG)
    s_new = jnp.sum(q * kn_ref[0], axis=1, keepdims=True) * scale + rb[:, 0:1]
    m = jnp.maximum(jnp.max(s, axis=1, keepdims=True), s_new)
    p = jnp.exp(s - m)
    p_new = jnp.exp(s_new - m)
    inv_l = 1.0 / (jnp.sum(p, axis=1, keepdims=True) + p_new)
    pn = p * inv_l

    ones = jnp.ones((SUBLANES, PAGE_SIZE), F32)
    out_rows = []
    for h in range(nh):
        acc = jnp.zeros((hd, PAGE_SIZE), F32)
        for j in range(np_):
            acc = acc + pn[h:h + 1, j * PAGE_SIZE:(j + 1) * PAGE_SIZE] * v_pages[j][0, 0, h]
        out_rows.append(_dot_nt(ones, acc, precision=HIGHEST)[0:1])
    o_ref[0] = (jnp.concatenate(out_rows, axis=0) + (p_new * inv_l) * vn_ref[0]).astype(o_ref.dtype)


def _moba_sample_call(q, k_new, v_new, cache_k, cache_v, layer, page_table, rel_bias):
    n, nh, hd = q.shape
    np_ = PAGES_PER_SEQ
    rb = jnp.pad(rel_bias.T, ((0, 0), (0, LANES - REL_BUCKETS)))
    ck_t = cache_k.transpose(0, 1, 3, 4, 2)
    cv_t = cache_v.transpose(0, 1, 3, 4, 2)
    tok = pl.BlockSpec((1, nh, hd), lambda i, pt: (i, 0, 0))
    page = lambda j: pl.BlockSpec((1, 1, nh, hd, PAGE_SIZE),
                                  lambda i, pt: (layer, pt[i * np_ + j], 0, 0, 0))
    return pl.pallas_call(
        _moba_sample_kernel,
        out_shape=jax.ShapeDtypeStruct((n, nh, hd), BF16),
        grid_spec=pltpu.PrefetchScalarGridSpec(
            num_scalar_prefetch=1,
            grid=(n,),
            in_specs=[tok, pl.BlockSpec((1, nh, hd, 1), lambda i, pt: (i, 0, 0, 0)), tok, tok,
                      pl.BlockSpec((nh, LANES), lambda i, pt: (0, 0))]
            + [page(j) for j in range(np_)] + [page(j) for j in range(np_)],
            out_specs=tok,
            scratch_shapes=[pltpu.VMEM((nh, PAST_LEN), F32)]),
        compiler_params=_cparams("arbitrary"),
        name="moba_sample",
    )(page_table.reshape(-1), q, q.reshape(n, nh, hd, 1), k_new, v_new, rb,
      *([ck_t] * np_), *([cv_t] * np_))


def _moe_layout(counts_f32, n_rows):
    counts = counts_f32.astype(I32)
    padded = (counts + EXPERT_ROWS - 1) // EXPERT_ROWS * EXPERT_ROWS
    ends = jnp.cumsum(padded)
    pstart = ends - padded
    nb = n_rows // EXPERT_ROWS
    block_start = jnp.arange(nb, dtype=I32) * EXPERT_ROWS
    block_e = jnp.minimum(jnp.sum((ends[None, :] <= block_start[:, None]).astype(I32), axis=1), N_EXPERTS - 1)
    n_used = (ends[-1:] // EXPERT_ROWS).astype(I32)
    return pstart.astype(I32), block_e.astype(I32), n_used


def _moe_rows(m):
    n_asg = m * TOP_K
    return -(-(n_asg + N_EXPERTS * (EXPERT_ROWS - 1)) // EXPERT_ROWS) * EXPERT_ROWS


MOE_TOKEN_TILE = 128


def _merge_group(ya, yb, gate, x, p, layer, lw):
    return _merge_call(ya, yb, gate, x, p, layer, lw['wug'], lw['wum'], lw['wo'], lw['wpg'], lw['wpp'],
                       lw['wrt'], lw['ln1_g'], lw['ln1_b'], _row_tile(x.shape[0], 2 * MERGE_SUB_ROWS))


def _moe_and_norm(groups, layer, lw):
    tn = MOE_TOKEN_TILE
    aff_t = jnp.concatenate([g[2] for g in groups], axis=1)
    eid, wt, pos, cnt = _route_call(aff_t, lw['b_router'], tn)
    n_rows = _moe_rows(aff_t.shape[1])
    pstart, block_e, n_used = _moe_layout(cnt[:, 0], n_rows)
    dest = _dest_call(pstart, eid, pos)
    tiles, t0 = [], 0
    for h, _, _ in groups:
        tiles.append(slice(t0, t0 + h.shape[0] // tn))
        t0 = tiles[-1].stop
    xs = jnp.zeros((n_rows, groups[0][0].shape[1]), F32)
    for (h, _, _), tl in zip(groups, tiles):
        xs = _dispatch_call(dest[tl], h, xs)
    ys = _expert_call(block_e, n_used, xs, lw['w_exp_gate'], lw['w_exp_up'], lw['w_exp_down'], layer)
    return [_combine_call(dest[tl], wt[tl], h, ple, ys, lw['ln2_g'], lw['ln2_b'])
            for (h, ple, _), tl in zip(groups, tiles)]


def kernel(x_prompt, x_sample, cache_k, cache_v, state_gdn, state_conv, page_table, p_prompt, p_sample,
           ln0_g, ln0_b, w_in, gdn_conv_w, gdn_a_log, gdn_dt_bias, gdn_norm_g, w_up_gdn, w_up_moba, w_o,
           ln1_g, ln1_b, rel_bias, w_router, b_router, w_exp_gate, w_exp_up, w_exp_down,
           w_ple_gate, w_ple_proj, ln2_g, ln2_b):
    assert GDN_DK == LANES and GDN_DV == LANES
    n_p, t_p, d = x_prompt.shape
    n_s, t_s, _ = x_sample.shape
    assert t_s == 1 and t_p % MOBA_BLOCK == 0 and n_s % SEQS_PER_STEP == 0
    m_p, m_s = n_p * t_p, n_s
    assert m_p % MOE_TOKEN_TILE == 0 and m_s % MOE_TOKEN_TILE == 0
    pp =p_prompt.reshape(DEPTH, m_p, PLE_DIM)
    ps = p_sample.reshape(DEPTH, m_s, PLE_DIM)
    heads = lambda a: a.reshape(a.shape[0], MOBA_HEADS, MOBA_HD)

    xp = _ln_call(x_prompt.reshape(m_p, d), ln0_g, ln0_b)
    xs = _ln_call(x_sample.reshape(m_s, d), ln0_g, ln0_b)
    wrt = w_router.T
    outs = {name: [] for name in ('kp', 'vp', 'sp', 'cp', 'ks', 'vs', 'ss', 'cs')}
    for i in range(DEPTH):
        w_parts = _split_w_in(w_in[i])
        lw = dict(wug=w_up_gdn[i].astype(BF16), wum=w_up_moba[i].astype(BF16), wo=w_o[i].astype(BF16),
                  wpg=w_ple_gate[i].astype(BF16), wpp=w_ple_proj[i].astype(BF16), wrt=wrt,
                  ln1_g=ln1_g[i], ln1_b=ln1_b[i], ln2_g=ln2_g[i], ln2_b=ln2_b[i], b_router=b_router,
                  w_exp_gate=w_exp_gate, w_exp_up=w_exp_up, w_exp_down=w_exp_down)

        conv, z, ab, q, gate, kt, vt = _proj_call(xp, w_parts, n_p)
        seq = lambda a: a.reshape(n_p, t_p, a.shape[-1])
        ya, s_new = _gdn_prompt_call(seq(conv), seq(z), seq(ab), gdn_conv_w[i], gdn_a_log[i],
                                     gdn_dt_bias[i], gdn_norm_g[i])
        yb = _moba_prompt_call(seq(q), kt, vt, rel_bias)
        group_p = _merge_group(ya.reshape(m_p, GDN_VW), yb.reshape(m_p, MOBA_W), gate, xp, pp, i, lw)
        untranspose = lambda a: a.reshape(n_p, MOBA_HEADS, MOBA_HD, t_p).transpose(0, 3, 1, 2)
        outs['kp'].append(untranspose(kt))
        outs['vp'].append(untranspose(vt))
        outs['sp'].append(s_new)
        outs['cp'].append(seq(conv)[:, t_p - (GDN_CONV - 1):, :])

        conv, z, ab, q, gate, kt, vt = _proj_call(xs, w_parts, 1)
        k = kt[0].T
        v = vt[0].T
        ya, s_new = _gdn_sample_call(conv, state_conv[i].transpose(1, 0, 2), z, ab, gdn_conv_w[i],
                                     gdn_a_log[i], gdn_dt_bias[i], gdn_norm_g[i], state_gdn, i)
        yb = _moba_sample_call(heads(q), heads(k), heads(v), cache_k, cache_v, i, page_table, rel_bias)
        group_s = _merge_group(ya, yb.reshape(m_s, MOBA_W), gate, xs, ps, i, lw)
        xp, xs = _moe_and_norm([group_p, group_s], i, lw)
        outs['ks'].append(k.reshape(n_s, 1, MOBA_HEADS, MOBA_HD))
        outs['vs'].append(v.reshape(n_s, 1, MOBA_HEADS, MOBA_HD))
        outs['ss'].append(s_new)
        outs['cs'].append(jnp.concatenate([state_conv[i][:, 1:, :], conv[:, None, :]], axis=1))

    st = lambda name: jnp.stack(outs[name])
    return (xp.reshape(n_p, t_p, d), xs.reshape(n_s, 1, d), st('kp'), st('vp'), st('sp'), st('cp'),
            st('ks'), st('vs'), st('ss'), st('cs'))
```

```python
import functools
import math

import numpy as np
import jax
import jax.numpy as jnp
from jax import lax
from jax.experimental import pallas as pl
from jax.experimental.pallas import tpu as pltpu

F32 = jnp.float32
BF16 = jnp.bfloat16
I32 = jnp.int32
HIGHEST = lax.Precision.HIGHEST

D_MODEL = 1024
DEPTH = 4
PAST_LEN = 2048
PAGE_SIZE = 128
GDN_HEADS = 4
GDN_DK = 128
GDN_DV = 128
GDN_CONV = 4
GDN_CHUNK = 64
MOBA_HEADS = 8
MOBA_HD = 64
MOBA_BLOCK = 256
MOBA_TOPK = 3
REL_BUCKETS = 32
REL_MAX_DIST = 128
N_EXPERTS = 32
N_GROUPS = 4
EXPERTS_PER_GROUP = N_EXPERTS // N_GROUPS
TOP_K = 2
D_FF_EXPERT = 512
PLE_DIM = 256
DN_ALPHA = (2 * DEPTH) ** 0.25
LN_EPS = 1e-5
NORM_EPS = 1e-6

GDN_QK = GDN_HEADS * GDN_DK
GDN_VW = GDN_HEADS * GDN_DV
GDN_CONV_CH = 2 * GDN_QK + GDN_VW
MOBA_W = MOBA_HEADS * MOBA_HD
COL_Z = GDN_CONV_CH
COL_A = COL_Z + GDN_VW
COL_B = COL_A + GDN_HEADS
COL_MOBA = COL_B + GDN_HEADS
COL_GATE = COL_MOBA + 3 * MOBA_W
N_IN = COL_GATE + 2 * D_MODEL

LANES = 128
SUBLANES = 8
VMEM_LIMIT_BYTES = 56 * 1024 * 1024

EXPERT_ROWS = 256
GDN_GROUP = 256
NEG_BIG = -1e30


def _cparams(*sem):
    return pltpu.CompilerParams(dimension_semantics=sem, vmem_limit_bytes=VMEM_LIMIT_BYTES)


def _row_tile(m, target):
    t = min(m, target)
    assert m % t == 0, (m, t)
    return t


def _t5_thresholds():
    exact = REL_BUCKETS // 2
    n = np.arange(0, 2 * REL_MAX_DIST)
    log_ratio = np.log(np.maximum(n, 1).astype(np.float32) / np.float32(exact)) / np.float32(
        math.log(REL_MAX_DIST / exact))
    large = exact + (log_ratio * np.float32(REL_BUCKETS - exact)).astype(np.int32)
    bucket = np.where(n < exact, n, np.minimum(large, REL_BUCKETS - 1))
    assert np.all(np.diff(bucket) >= 0)
    return [int(np.argmax(bucket >= k)) for k in range(REL_BUCKETS)]


T5_THRESHOLDS = _t5_thresholds()


def _layer_norm(x, g, b):
    mu = jnp.mean(x, axis=-1, keepdims=True)
    xc = x - mu
    var = jnp.mean(xc * xc, axis=-1, keepdims=True)
    return xc * lax.rsqrt(var + LN_EPS) * g + b


def _sigmoid(x):
    return 1.0 / (1.0 + jnp.exp(-x))


def _silu(x):
    return x * _sigmoid(x)


def _softplus(x):
    return jnp.maximum(x, 0.0) + jnp.log(1.0 + jnp.exp(-jnp.abs(x)))


def _dot(a, b):
    return jnp.dot(a.astype(BF16), b.astype(BF16), preferred_element_type=F32)


def _dot_nt(a, b, precision=None):
    if precision is None:
        a, b = a.astype(BF16), b.astype(BF16)
    return lax.dot_general(a, b, (((1,), (1,)), ((), ())), precision=precision,
                           preferred_element_type=F32)


def _dot_tn(a, b, precision=None):
    if precision is None:
        a, b = a.astype(BF16), b.astype(BF16)
    return lax.dot_general(a, b, (((0,), (0,)), ((), ())), precision=precision,
                           preferred_element_type=F32)


def _ln_kernel(x_ref, g_ref, b_ref, o_ref):
    o_ref[...] = _layer_norm(x_ref[...], g_ref[...], b_ref[...])


def _ln_call(x, g, b):
    m, d = x.shape
    tm = _row_tile(m, 512)
    return pl.pallas_call(
        _ln_kernel,
        out_shape=jax.ShapeDtypeStruct((m, d), F32),
        grid=(m // tm,),
        in_specs=[pl.BlockSpec((tm, d), lambda i: (i, 0)),
                  pl.BlockSpec((1, d), lambda i: (0, 0)),
                  pl.BlockSpec((1, d), lambda i: (0, 0))],
        out_specs=pl.BlockSpec((tm, d), lambda i: (i, 0)),
        compiler_params=_cparams("parallel"),
        name="ln0",
    )(x, g.reshape(1, d), b.reshape(1, d))


N_ROW_PARTS = 5


def _proj_kernel(x_ref, wc_ref, wz_ref, wab_ref, wq_ref, wg_ref, wkt_ref, wvt_ref,
                 conv_ref, z_ref, ab_ref, q_ref, gate_ref, kt_ref, vt_ref):
    x = x_ref[...].astype(BF16)
    for w_ref, o_ref in ((wc_ref, conv_ref), (wz_ref, z_ref), (wab_ref, ab_ref), (wq_ref, q_ref),
                         (wg_ref, gate_ref)):
        o_ref[...] = jnp.dot(x, w_ref[...], preferred_element_type=F32)
    kt_ref[0] = _dot_nt(wkt_ref[...], x)
    vt_ref[0] = _dot_nt(wvt_ref[...], x)


def _split_w_in(w):
    wab = jnp.pad(w[:, COL_A:COL_MOBA], ((0, 0), (0, LANES - 2 * GDN_HEADS)))
    parts = (w[:, :COL_Z], w[:, COL_Z:COL_A], wab, w[:, COL_MOBA:COL_MOBA + MOBA_W], w[:, COL_GATE:],
             w[:, COL_MOBA + MOBA_W:COL_MOBA + 2 * MOBA_W].T, w[:, COL_MOBA + 2 * MOBA_W:COL_GATE].T)
    return tuple(p.astype(BF16) for p in parts)


def _proj_call(x, w_parts, n_seq):
    m, d = x.shape
    t_len = m // n_seq
    tm = _row_tile(t_len, 256)
    tiles = t_len // tm
    row_w = [p.shape[1] for p in w_parts[:N_ROW_PARTS]]
    col_w = [p.shape[0] for p in w_parts[N_ROW_PARTS:]]
    return pl.pallas_call(
        _proj_kernel,
        out_shape=tuple(jax.ShapeDtypeStruct((m, n), F32) for n in row_w)
        + tuple(jax.ShapeDtypeStruct((n_seq, n, t_len), F32) for n in col_w),
        grid=(m // tm,),
        in_specs=[pl.BlockSpec((tm, d), lambda i: (i, 0))]
        + [pl.BlockSpec(p.shape, lambda i: (0, 0)) for p in w_parts],
        out_specs=tuple(pl.BlockSpec((tm, n), lambda i: (i, 0)) for n in row_w)
        + tuple(pl.BlockSpec((1, n, tm), lambda i: (i // tiles, 0, i % tiles)) for n in col_w),
        compiler_params=_cparams("parallel"),
        name="proj",
    )(x, *w_parts)


def _chunk_cumsum(x, pos_in_chunk, axis):
    s = 1
    while s < GDN_CHUNK:
        x = x + jnp.where(pos_in_chunk >= s, pltpu.roll(x, s, axis), 0.0)
        s *= 2
    return x


def _l2_normalize(x):
    return x * lax.rsqrt(jnp.sum(x * x, axis=-1, keepdims=True) + NORM_EPS)


def _gdn_gates(ab, alog_row, dtb_row):
    g = -jnp.exp(alog_row) * _softplus(ab + dtb_row)
    return g, _sigmoid(ab)


GDN_GROUPS_PER_ITER = 4


def _gdn_prompt_kernel(conv_ref, z_ref, ab_ref, w_ref, alog_ref, dtb_ref, ng_ref, y_ref, s_ref,
                       qn_s, kn_s, kb_s, vb_s, gcb_s, gcr_s, u_s, w_s, qg_s, kg_s, intra_s, egl_s):
    t_len = conv_ref.shape[1]
    n_groups = t_len // GDN_GROUP
    n_chunks = t_len // GDN_CHUNK
    cpg = GDN_GROUP // GDN_CHUNK
    groups_per_iter = math.gcd(n_groups, GDN_GROUPS_PER_ITER)

    row = lax.broadcasted_iota(I32, (t_len, LANES), 0)
    lane = lax.broadcasted_iota(I32, (t_len, LANES), 1)

    ab_t = ab_ref[0].T[0:SUBLANES]
    g_rows = -jnp.exp(alog_ref[:, 0:1]) * _softplus(ab_t + dtb_ref[:, 0:1])
    lane_t = lax.broadcasted_iota(I32, (SUBLANES, t_len), 1)
    sub_t = lax.broadcasted_iota(I32, (SUBLANES, t_len), 0)
    gc_rows = _chunk_cumsum(g_rows, lane_t % GDN_CHUNK, 1)
    packed = jnp.where(sub_t < GDN_HEADS, gc_rows, _sigmoid(ab_t))
    cols_all = jnp.concatenate([packed, jnp.zeros((LANES - SUBLANES, t_len), F32)], axis=0).T

    ii = lax.broadcasted_iota(I32, (GDN_GROUP, GDN_GROUP), 0)
    jj = lax.broadcasted_iota(I32, (GDN_GROUP, GDN_GROUP), 1)
    same = (ii // GDN_CHUNK) == (jj // GDN_CHUNK)
    incl = same & (ii >= jj)
    strict = same & (ii > jj)
    eye = (ii == jj).astype(F32)

    def conv(col0, h):
        cols = pl.ds(pl.multiple_of(col0 + h * LANES, LANES), LANES)
        x = conv_ref[0, :, cols]
        w = w_ref[:, cols]
        taps = [w[GDN_CONV - 1 - j:GDN_CONV - j] for j in range(GDN_CONV)]
        rolled = [pltpu.roll(x, j, 0) for j in range(1, GDN_CONV)]
        y = x * taps[0]
        for j in range(1, GDN_CONV):
            y = y + rolled[j - 1] * taps[j]
        top = x[:SUBLANES] * taps[0]
        for j in range(1, GDN_CONV):
            top = top + jnp.where(row[:SUBLANES] >= j, rolled[j - 1][:SUBLANES], 0.0) * taps[j]
        return _silu(jnp.concatenate([top, y[SUBLANES:]], axis=0))

    def head_body(h, carry):
        qn = _l2_normalize(conv(0, h)) * (GDN_DK ** -0.5)
        kn = _l2_normalize(conv(GDN_QK, h))
        vv = conv(2 * GDN_QK, h)
        gc_col = jnp.sum(jnp.where(lane == h, cols_all, 0.0), axis=1, keepdims=True)
        beta_col = jnp.sum(jnp.where(lane == h + GDN_HEADS, cols_all, 0.0), axis=1, keepdims=True)
        gc_row = jnp.sum(jnp.where(sub_t == h, gc_rows, 0.0), axis=0, keepdims=True)
        qn_s[...] = qn
        kn_s[...] = kn
        kb_s[...] = kn * beta_col
        vb_s[...] = vv * beta_col
        gcb_s[...] = jnp.broadcast_to(gc_col, (t_len, LANES))
        for g in range(n_groups):
            gcr_s[g] = jnp.broadcast_to(gc_row[:, g * GDN_GROUP:(g + 1) * GDN_GROUP], (SUBLANES, GDN_GROUP))

        def group_iter(it, c2):
            gs = [it * groups_per_iter + gg for gg in range(groups_per_iter)]
            r0s = [pl.multiple_of(g * GDN_GROUP, GDN_GROUP) for g in gs]
            rows = [pl.ds(r0, GDN_GROUP) for r0 in r0s]
            q_g = [qn_s[r, :] for r in rows]
            k_g = [kn_s[r, :] for r in rows]
            kb_g = [kb_s[r, :] for r in rows]
            vb_g = [vb_s[r, :] for r in rows]
            gcb_g = [gcb_s[r, :] for r in rows]
            gcol = [x[:, 0:1] for x in gcb_g]
            eg = [jnp.exp(x) for x in gcol]
            decay = [jnp.where(incl, jnp.exp(jnp.where(incl, gc - gcr_s[g][0:1, :], 0.0)), 0.0)
                     for gc, g in zip(gcol, gs)]
            kk = [_dot_nt(a, b) for a, b in zip(kb_g, k_g)]
            qk = [_dot_nt(a, b) for a, b in zip(q_g, k_g)]
            a_mat = [jnp.where(strict, x * d, 0.0) for x, d in zip(kk, decay)]
            intra = [x * d for x, d in zip(qk, decay)]
            t_inv = [eye - a for a in a_mat]
            x_pow = a_mat
            p = 2
            while p < GDN_CHUNK:
                x_pow = [_dot(x, x) for x in x_pow]
                t_inv = [t + _dot(t, x) for t, x in zip(t_inv, x_pow)]
                p *= 2
            uw = [_dot(t, jnp.concatenate([vb, kb * e], axis=1)) for t, vb, kb, e in zip(t_inv, vb_g, kb_g, eg)]
            for i, g in enumerate(gs):
                u_s[h, rows[i], :] = uw[i][:, :GDN_DV]
                w_s[h, rows[i], :] = uw[i][:, GDN_DV:].astype(w_s.dtype)
                qg_s[h, rows[i], :] = (q_g[i] * eg[i]).astype(qg_s.dtype)
                for c in range(cpg):
                    lo = c * GDN_CHUNK
                    g_last = gcb_g[i][lo + GDN_CHUNK - 1:lo + GDN_CHUNK, :]
                    rows_c = pl.ds(pl.multiple_of(r0s[i] + lo, GDN_CHUNK), GDN_CHUNK)
                    kg = k_g[i][lo:lo + GDN_CHUNK] * jnp.exp(g_last - gcb_g[i][lo:lo + GDN_CHUNK])
                    kg_s[h, rows_c, :] = kg.astype(kg_s.dtype)
                    intra_s[h, rows_c, :] = intra[i][lo:lo + GDN_CHUNK, lo:lo + GDN_CHUNK].astype(intra_s.dtype)
                    egl_s[h, g * cpg + c] = jnp.broadcast_to(jnp.exp(g_last), (SUBLANES, LANES))
            return c2

        lax.fori_loop(0, n_groups // groups_per_iter, group_iter, 0)
        return carry

    lax.fori_loop(0, GDN_HEADS, head_body, 0)

    ng = ng_ref[...]

    def chunk_body(c, s_mats):
        r0 = pl.multiple_of(c * GDN_CHUNK, GDN_CHUNK)
        rows = pl.ds(r0, GDN_CHUNK)
        heads = range(GDN_HEADS)
        wq = [jnp.concatenate([w_s[h, rows, :], qg_s[h, rows, :]], axis=0) for h in heads]
        ws_qs = [_dot(wq[h], s_mats[h]) for h in heads]
        v_new = [u_s[h, rows, :] - ws_qs[h][:GDN_CHUNK] for h in heads]
        s_out = [s_mats[h] * egl_s[h, c][0:1, :] + _dot_tn(kg_s[h, rows, :], v_new[h]) for h in heads]
        o_all = [ws_qs[h][GDN_CHUNK:] + _dot(intra_s[h, rows, :], v_new[h]) for h in heads]
        for h in heads:
            cols = slice(h * GDN_DV, (h + 1) * GDN_DV)
            o = o_all[h]
            o = o * lax.rsqrt(jnp.mean(o * o, axis=-1, keepdims=True) + NORM_EPS) * ng * _silu(z_ref[0, rows, cols])
            y_ref[0, rows, cols] = o.astype(y_ref.dtype)
        return tuple(s_out)

    zero = jnp.zeros((GDN_DK, GDN_DV), F32)
    s_fin = lax.fori_loop(0, n_chunks, chunk_body, (zero,) * GDN_HEADS)
    for h in range(GDN_HEADS):
        s_ref[0, h] = s_fin[h]


def _gdn_prompt_call(conv, z, ab, conv_w, a_log, dt_bias, norm_g):
    n, t_len, c = conv.shape
    tile = lambda v: jnp.broadcast_to(jnp.pad(v, (0, SUBLANES - GDN_HEADS))[:, None], (SUBLANES, LANES))
    once = pl.Buffered(1)
    small = lambda shape: pl.BlockSpec(shape, lambda i: (0, 0))
    per_head = lambda w, dt: pltpu.VMEM((GDN_HEADS, t_len, w), dt)
    big = pltpu.VMEM((t_len, LANES), F32)
    return pl.pallas_call(
        _gdn_prompt_kernel,
        out_shape=(jax.ShapeDtypeStruct((n, t_len, GDN_VW), BF16),
                   jax.ShapeDtypeStruct((n, GDN_HEADS, GDN_DK, GDN_DV), F32)),
        grid=(n,),
        in_specs=[pl.BlockSpec((1, t_len, c), lambda i: (i, 0, 0), pipeline_mode=once),
                  pl.BlockSpec((1, t_len, GDN_VW), lambda i: (i, 0, 0), pipeline_mode=once),
                  pl.BlockSpec((1, t_len, LANES), lambda i: (i, 0, 0)),
                  small((GDN_CONV, c)), small((SUBLANES, LANES)), small((SUBLANES, LANES)),
                  small((1, GDN_DV))],
        out_specs=(pl.BlockSpec((1, t_len, GDN_VW), lambda i: (i, 0, 0)),
                   pl.BlockSpec((1, GDN_HEADS, GDN_DK, GDN_DV), lambda i: (i, 0, 0, 0))),
        scratch_shapes=[big, big, big, big, big,
                        pltpu.VMEM((t_len // GDN_GROUP, SUBLANES, GDN_GROUP), F32),
                        per_head(GDN_DV, F32), per_head(GDN_DK, BF16), per_head(GDN_DK, BF16),
                        per_head(GDN_DK, BF16), per_head(GDN_CHUNK, BF16),
                        pltpu.VMEM((GDN_HEADS, t_len // GDN_CHUNK, SUBLANES, LANES), F32)],
        compiler_params=_cparams("parallel"),
        name="gdn_prompt",
    )(conv, z, ab, conv_w, tile(a_log), tile(dt_bias), norm_g.reshape(1, GDN_DV))


HEADS_PER_TILE = LANES // MOBA_HD


def _t5_bias(dist, rb_ref, head):
    b = jnp.full(dist.shape, rb_ref[head, 0], F32)
    for k in range(1, REL_BUCKETS):
        b = jnp.where(dist >= T5_THRESHOLDS[k], rb_ref[head, k], b)
    return b


def _top_blocks(gate_t, n_valid):
    nq = gate_t.shape[1]
    sub = lax.broadcasted_iota(I32, gate_t.shape, 0)
    g = jnp.where(sub < n_valid, gate_t, -jnp.inf)
    rank = jnp.zeros(g.shape, F32)
    for b2 in range(SUBLANES):
        row = g[b2:b2 + 1, :]
        rank = rank + jnp.where(row > g, 1.0, 0.0) + jnp.where(row == g, jnp.where(sub > b2, 1.0, 0.0), 0.0)
    sel_t = jnp.where((rank < MOBA_TOPK) & (sub < n_valid), 1.0, 0.0)
    return jnp.concatenate([sel_t, jnp.zeros((LANES - SUBLANES, nq), F32)], axis=0).T


def _moba_prompt_kernel(rb_ref, q_ref, kt_ref, vt_ref, o_ref, kmean_s, biasd_s, biasp_s, logit_s, m_s,
                        acc_s):
    hp = pl.program_id(0)
    n = pl.program_id(1)
    qb = pl.program_id(2)
    blk = MOBA_BLOCK
    n_blocks = kt_ref.shape[2] // blk
    half = blk // 2
    ii = lax.broadcasted_iota(I32, (blk, blk), 0)
    jj = lax.broadcasted_iota(I32, (blk, blk), 1)
    lane = lax.broadcasted_iota(I32, (blk, LANES), 1)

    @pl.when((n == 0) & (qb == 0))
    def _():
        for e in range(HEADS_PER_TILE):
            head = hp * HEADS_PER_TILE + e
            biasd_s[e] = jnp.where(ii >= jj, _t5_bias(ii - jj, rb_ref, head), NEG_BIG)
            biasp_s[e] = _t5_bias(blk + ii - jj, rb_ref, head)

    @pl.when(qb == 0)
    def _():
        cols = [jnp.mean(kt_ref[0, :, b * blk:(b + 1) * blk], axis=1, keepdims=True) for b in range(n_blocks)]
        cols.append(jnp.zeros((LANES, LANES - n_blocks), F32))
        kmean_s[...] = jnp.concatenate(cols, axis=1).T

    q2 = q_ref[0]
    own_cols = pl.ds(pl.multiple_of(qb * blk, blk), blk)
    kt_own = kt_ref[0, :, own_cols].astype(BF16)
    scale = MOBA_HD ** -0.5

    q_scaled, sels, fars, m_init = [], [], [], []
    for e in range(HEADS_PER_TILE):
        qh = jnp.where(lane // MOBA_HD == e, q2, 0.0)
        gate_t = _dot_nt(kmean_s[0:SUBLANES, :], qh, precision=HIGHEST)
        sels.append(_top_blocks(gate_t, qb))
        fars.append(rb_ref[hp * HEADS_PER_TILE + e, REL_BUCKETS - 1])
        qs = (qh * scale).astype(BF16)
        q_scaled.append(qs)
        s = jnp.dot(qs, kt_own, preferred_element_type=F32) + biasd_s[e]
        logit_s[e, :, own_cols] = s
        m_init.append(jnp.maximum(s[:, :half], s[:, half:]))
    heads = range(HEADS_PER_TILE)

    def masked_logits(bs, tile_bias):
        cols = [pl.ds(pl.multiple_of(b * blk, blk), blk) for b in bs]
        kt_b = [kt_ref[0, :, c].astype(BF16) for c in cols]
        raw = [[jnp.dot(q_scaled[e], k, preferred_element_type=F32) for e in heads] for k in kt_b]
        picked = [[jnp.sum(jnp.where(lane == b, sels[e], 0.0), axis=-1, keepdims=True) for e in heads]
                  for b in bs]
        m_out = [None] * HEADS_PER_TILE
        for i in range(len(bs)):
            for e in heads:
                if tile_bias[e].ndim == 0:
                    s = raw[i][e] + jnp.where(picked[i][e] > 0.0, tile_bias[e], NEG_BIG)
                else:
                    s = raw[i][e] + tile_bias[e] + jnp.where(picked[i][e] > 0.0, 0.0, NEG_BIG)
                logit_s[e, :, cols[i]] = s
                m_blk = jnp.maximum(s[:, :half], s[:, half:])
                m_out[e] = m_blk if m_out[e] is None else jnp.maximum(m_out[e], m_blk)
        return m_out

    n_far = jnp.maximum(qb - 1, 0)

    def far_pair(it, m_parts):
        m_blk = masked_logits([2 * it, 2 * it + 1], fars)
        return tuple(jnp.maximum(m_parts[e], m_blk[e]) for e in heads)

    m_parts = lax.fori_loop(0, n_far // 2, far_pair, tuple(m_init))
    for e in heads:
        m_s[e] = m_parts[e]

    @pl.when(n_far % 2 == 1)
    def _():
        m_blk = masked_logits([n_far - 1], fars)
        for e in heads:
            m_s[e] = jnp.maximum(m_s[e], m_blk[e])

    @pl.when(qb >= 1)
    def _():
        m_blk = masked_logits([qb - 1], [biasp_s[e] for e in heads])
        for e in heads:
            m_s[e] = jnp.maximum(m_s[e], m_blk[e])

    m_rows = [jnp.max(m_s[e], axis=-1, keepdims=True) for e in heads]

    row = lax.broadcasted_iota(I32, (LANES, blk), 0)
    ones_row = [((e + 1) % HEADS_PER_TILE) * MOBA_HD for e in heads]

    def weigh_blocks(bs, accs):
        cols = [pl.ds(pl.multiple_of(b * blk, blk), blk) for b in bs]
        vt_b = [vt_ref[0, :, c] for c in cols]
        p = [[jnp.exp(logit_s[e, :, c] - m_rows[e]) for e in heads] for c in cols]
        vt_aug = [[jnp.where(row // MOBA_HD == e, v, jnp.where(row == ones_row[e], 1.0, 0.0)) for e in heads]
                  for v in vt_b]
        pv = [[_dot_nt(p[i][e], vt_aug[i][e]) for e in heads] for i in range(len(bs))]
        out = list(accs)
        for i in range(len(bs)):
            for e in heads:
                out[e] = out[e] + pv[i][e]
        return tuple(out)

    n_vis = qb + 1
    zero = jnp.zeros((blk, LANES), F32)
    accs = lax.fori_loop(0, n_vis // 2, lambda it, a: weigh_blocks([2 * it, 2 * it + 1], a),
                         (zero,) * HEADS_PER_TILE)
    for e in heads:
        acc_s[e] = accs[e]

    @pl.when(n_vis % 2 == 1)
    def _():
        last = weigh_blocks([qb], tuple(acc_s[e] for e in heads))
        for e in heads:
            acc_s[e] = last[e]

    fin = [acc_s[e] for e in heads]
    res = fin[0] / fin[0][:, ones_row[0]:ones_row[0] + 1]
    for e in range(1, HEADS_PER_TILE):
        res = jnp.where(lane // MOBA_HD == e, fin[e] / fin[e][:, ones_row[e]:ones_row[e] + 1], res)
    o_ref[0] = res.astype(o_ref.dtype)


def _moba_prompt_call(q, kt, vt, rel_bias):
    n, t_len, _ = q.shape
    blk = MOBA_BLOCK
    assert blk == 2 * LANES and t_len // blk <= SUBLANES
    n_tiles = MOBA_W // LANES
    kv_spec = pl.BlockSpec((1, LANES, t_len), lambda hp, i, qb, rb: (i, hp, 0))
    q_spec = pl.BlockSpec((1, blk, LANES), lambda hp, i, qb, rb: (i, qb, hp))
    return pl.pallas_call(
        _moba_prompt_kernel,
        out_shape=jax.ShapeDtypeStruct((n, t_len, MOBA_W), BF16),
        grid_spec=pltpu.PrefetchScalarGridSpec(
            num_scalar_prefetch=1,
            grid=(n_tiles, n, t_len // blk),
            in_specs=[q_spec, kv_spec, kv_spec],
            out_specs=q_spec,
            scratch_shapes=[pltpu.VMEM((LANES, LANES), F32),
                            pltpu.VMEM((HEADS_PER_TILE, blk, blk), F32),
                            pltpu.VMEM((HEADS_PER_TILE, blk, blk), F32),
                            pltpu.VMEM((HEADS_PER_TILE, blk, t_len), F32),
                            pltpu.VMEM((HEADS_PER_TILE, blk, LANES), F32),
                            pltpu.VMEM((HEADS_PER_TILE, blk, LANES), F32)]),
        compiler_params=_cparams("arbitrary", "arbitrary", "arbitrary"),
        name="moba_prompt",
    )(rel_bias.T, q, kt, vt)


MERGE_SUB_ROWS = 256


def _merge_kernel(ya_ref, yb_ref, gate_ref, x_ref, p_ref, wug_ref, wum_ref, wo_ref, wpg_ref, wpp_ref,
                  wrt_ref, g_ref, b_ref, h_ref, ple_ref, aff_ref):
    tm = x_ref.shape[0]
    sub = min(tm, MERGE_SUB_ROWS)
    parts = [slice(k * sub, (k + 1) * sub) for k in range(tm // sub)]
    up_a = [jnp.dot(ya_ref[r, :], wug_ref[...], preferred_element_type=F32) for r in parts]
    up_b = [jnp.dot(yb_ref[r, :], wum_ref[...], preferred_element_type=F32) for r in parts]
    merged = [_sigmoid(gate_ref[r, :D_MODEL]) * a + _sigmoid(gate_ref[r, D_MODEL:]) * b
              for r, a, b in zip(parts, up_a, up_b)]
    proj = [_dot(mg, wo_ref[...]) for mg in merged]
    h = [_layer_norm(DN_ALPHA * x_ref[r, :] + o, g_ref[...], b_ref[...]) for r, o in zip(parts, proj)]
    for r, hh in zip(parts, h):
        h_ref[r, :] = hh
    ple_gate = [_dot(hh, wpg_ref[...]) for hh in h]
    ple_val = [_dot(p_ref[0, r, :], wpp_ref[...]) for r in parts]
    for r, pg, pv in zip(parts, ple_gate, ple_val):
        ple_ref[r, :] = _sigmoid(pg) * pv
    for r, hh in zip(parts, h):
        aff_ref[:, r] = _sigmoid(_dot_nt(wrt_ref[...], hh, precision=HIGHEST))


def _merge_call(ya, yb, gate, x, p, layer, wug, wum, wo, wpg, wpp, wrt, ln_g, ln_b, tm):
    m, d = x.shape
    rows = lambda n: pl.BlockSpec((tm, n), lambda i: (i, 0))
    full = lambda a: pl.BlockSpec(a.shape, lambda i: (0, 0))
    return pl.pallas_call(
        _merge_kernel,
        out_shape=(jax.ShapeDtypeStruct((m, d), F32), jax.ShapeDtypeStruct((m, d), F32),
                   jax.ShapeDtypeStruct((N_EXPERTS, m), F32)),
        grid=(m // tm,),
        in_specs=[rows(GDN_VW), rows(MOBA_W), rows(2 * d), rows(d),
                  pl.BlockSpec((1, tm, PLE_DIM), lambda i: (layer, i, 0)),
                  full(wug), full(wum), full(wo), full(wpg), full(wpp), full(wrt),
                  pl.BlockSpec((1, d), lambda i: (0, 0)), pl.BlockSpec((1, d), lambda i: (0, 0))],
        out_specs=(rows(d), rows(d), pl.BlockSpec((N_EXPERTS, tm), lambda i: (0, i))),
        compiler_params=_cparams("parallel"),
        name="merge",
    )(ya, yb, gate, x, p, wug, wum, wo, wpg, wpp, wrt, ln_g.reshape(1, d), ln_b.reshape(1, d))


def _route_kernel(aff_ref, br_ref, eid_ref, wt_ref, pos_ref, cnt_ref, carry_s, tri_s):
    i = pl.program_id(0)
    tn = aff_ref.shape[1]

    @pl.when(i == 0)
    def _():
        carry_s[...] = jnp.zeros(carry_s.shape, F32)
        r = lax.broadcasted_iota(I32, (tn, tn), 0)
        c = lax.broadcasted_iota(I32, (tn, tn), 1)
        tri_s[...] = jnp.where(r < c, 1.0, 0.0).astype(tri_s.dtype)

    aff = aff_ref[...]
    sel = aff + br_ref[...]
    gsz = EXPERTS_PER_GROUP
    sub = lax.broadcasted_iota(I32, (gsz, tn), 0)
    best = None
    for g in range(N_GROUPS):
        v = sel[g * gsz:(g + 1) * gsz]
        a = aff[g * gsz:(g + 1) * gsz]
        m1 = jnp.max(v, axis=0, keepdims=True)
        i1 = jnp.min(jnp.where(v == m1, sub, gsz), axis=0, keepdims=True)
        v2 = jnp.where(sub == i1, -jnp.inf, v)
        m2 = jnp.max(v2, axis=0, keepdims=True)
        i2 = jnp.min(jnp.where(v2 == m2, sub, gsz), axis=0, keepdims=True)
        a1 = jnp.sum(jnp.where(sub == i1, a, 0.0), axis=0, keepdims=True)
        a2 = jnp.sum(jnp.where(sub == i2, a, 0.0), axis=0, keepdims=True)
        cand = (m1 + m2, i1 + g * gsz, i2 + g * gsz, a1, a2)
        if best is None:
            best = cand
        else:
            better = cand[0] > best[0]
            best = tuple(jnp.where(better, c, b) for c, b in zip(cand, best))
    _, e1, e2, a1, a2 = best
    eid_ref[0, 0:1, :] = e1
    eid_ref[0, 1:2, :] = e2
    wt_ref[0, 0:1, :] = a1 / (a1 + a2)
    wt_ref[0, 1:2, :] = a2 / (a1 + a2)

    e_iota = lax.broadcasted_iota(I32, (N_EXPERTS, tn), 0)
    oh1 = jnp.where(e_iota == e1, 1.0, 0.0)
    oh2 = jnp.where(e_iota == e2, 1.0, 0.0)
    tri = tri_s[...]
    tot1 = jnp.sum(oh1, axis=1, keepdims=True)
    tot2 = jnp.sum(oh2, axis=1, keepdims=True)
    base = carry_s[:, 0:1]
    c1 = base + jnp.dot(oh1.astype(tri.dtype), tri, preferred_element_type=F32)
    c2 = base + tot1 + jnp.dot(oh2.astype(tri.dtype), tri, preferred_element_type=F32)
    pos_ref[0, 0:1, :] = jnp.sum(oh1 * c1, axis=0, keepdims=True).astype(I32)
    pos_ref[0, 1:2, :] = jnp.sum(oh2 * c2, axis=0, keepdims=True).astype(I32)
    carry_s[...] = carry_s[...] + (tot1 + tot2)
    cnt_ref[...] = carry_s[...]


def _route_call(aff_t, b_router, tn):
    m = aff_t.shape[1]
    nt = m // tn
    tok = lambda dt: jax.ShapeDtypeStruct((nt, TOP_K, tn), dt)
    tok_spec = pl.BlockSpec((1, TOP_K, tn), lambda i: (i, 0, 0))
    return pl.pallas_call(
        _route_kernel,
        out_shape=(tok(I32), tok(F32), tok(I32), jax.ShapeDtypeStruct((N_EXPERTS, LANES), F32)),
        grid=(nt,),
        in_specs=[pl.BlockSpec((N_EXPERTS, tn), lambda i: (0, i)),
                  pl.BlockSpec((N_EXPERTS, 1), lambda i: (0, 0))],
        out_specs=(tok_spec, tok_spec, tok_spec, pl.BlockSpec((N_EXPERTS, LANES), lambda i: (0, 0))),
        scratch_shapes=[pltpu.VMEM((N_EXPERTS, LANES), F32), pltpu.VMEM((tn, tn), BF16)],
        compiler_params=_cparams("arbitrary"),
        name="route",
    )(aff_t, b_router.reshape(N_EXPERTS, 1))


ROW_DMA_UNROLL = 8


def _dest_kernel(pstart_ref, eid_ref, pos_ref, dest_ref):
    eid = eid_ref[...]
    dest = pos_ref[...]
    for e in range(N_EXPERTS):
        dest = dest + jnp.where(eid == e, pstart_ref[e], 0)
    dest_ref[...] = dest


def _dest_call(pstart, eid, pos):
    nt, _, tm = eid.shape
    per_step = max(k for k in range(1, 65) if nt % k == 0)
    tok = pl.BlockSpec((per_step, TOP_K, tm), lambda i, ps: (i, 0, 0))
    return pl.pallas_call(
        _dest_kernel,
        out_shape=jax.ShapeDtypeStruct(eid.shape, I32),
        grid_spec=pltpu.PrefetchScalarGridSpec(num_scalar_prefetch=1, grid=(nt // per_step,),
                                               in_specs=[tok, tok], out_specs=tok),
        compiler_params=_cparams("parallel"),
        name="dest",
    )(pstart, eid, pos)


def _dispatch_kernel(dest_ref, h_ref, xs_in_ref, xs_ref, sem):
    del xs_in_ref
    tm = h_ref.shape[0]

    def issue(it, c):
        for u in range(ROW_DMA_UNROLL):
            t = it * ROW_DMA_UNROLL + u
            for s in range(TOP_K):
                pltpu.make_async_copy(h_ref.at[pl.ds(t, 1)], xs_ref.at[pl.ds(dest_ref[0, s, t], 1)], sem).start()
        return c

    lax.fori_loop(0, tm // ROW_DMA_UNROLL, issue, 0)
    for s in range(TOP_K):
        pltpu.make_async_copy(h_ref, xs_ref.at[pl.ds(0, tm)], sem).wait()


def _dispatch_call(dest, h, xs_init):
    m, d = h.shape
    nt, _, tm = dest.shape
    n_rows = xs_init.shape[0]
    assert tm % ROW_DMA_UNROLL == 0 and nt * tm == m
    return pl.pallas_call(
        _dispatch_kernel,
        out_shape=jax.ShapeDtypeStruct((n_rows, d), F32),
        grid=(nt,),
        in_specs=[pl.BlockSpec((1, TOP_K, tm), lambda i: (i, 0, 0), memory_space=pltpu.SMEM),
                  pl.BlockSpec((tm, d), lambda i: (i, 0)),
                  pl.BlockSpec(memory_space=pl.ANY)],
        out_specs=pl.BlockSpec(memory_space=pl.ANY),
        scratch_shapes=[pltpu.SemaphoreType.DMA(())],
        input_output_aliases={2: 0},
        compiler_params=_cparams("arbitrary"),
        name="dispatch",
    )(dest, h, xs_init)


EXPERT_FF_SPLIT = 2


def _expert_kernel(be_ref, nused_ref, xs_ref, wg_ref, wu_ref, wd_ref, ys_ref, wg_s, wu_s, wd_s):
    j = pl.program_id(0)

    @pl.when(j < nused_ref[0])
    def _():
        @pl.when((j == 0) | (be_ref[j] != be_ref[jnp.maximum(j - 1, 0)]))
        def _():
            wg_s[...] = wg_ref[0, 0].astype(wg_s.dtype)
            wu_s[...] = wu_ref[0, 0].astype(wu_s.dtype)
            wd_s[...] = wd_ref[0, 0].astype(wd_s.dtype)

        x = xs_ref[...].astype(BF16)
        fw = wg_s.shape[1] // EXPERT_FF_SPLIT
        cuts = [slice(k * fw, (k + 1) * fw) for k in range(EXPERT_FF_SPLIT)]
        gate = [_dot(x, wg_s[:, c]) for c in cuts]
        up = [_dot(x, wu_s[:, c]) for c in cuts]
        hid = [_silu(g) * u for g, u in zip(gate, up)]
        down = [_dot(hh, wd_s[c, :]) for hh, c in zip(hid, cuts)]
        ys_ref[...] = functools.reduce(lambda a, b: a + b, down)

    @pl.when(j >= nused_ref[0])
    def _():
        ys_ref[...] = jnp.zeros(ys_ref.shape, F32)


def _expert_call(block_e, n_used, xs, w_gate, w_up, w_down, layer):
    n_rows, d = xs.shape
    f = w_gate.shape[3]
    nb = n_rows // EXPERT_ROWS
    return pl.pallas_call(
        _expert_kernel,
        out_shape=jax.ShapeDtypeStruct((n_rows, d), F32),
        grid_spec=pltpu.PrefetchScalarGridSpec(
            num_scalar_prefetch=2,
            grid=(nb,),
            in_specs=[pl.BlockSpec((EXPERT_ROWS, d), lambda j, be, nu: (j, 0)),
                      pl.BlockSpec((1, 1, d, f), lambda j, be, nu: (layer, be[j], 0, 0)),
                      pl.BlockSpec((1, 1, d, f), lambda j, be, nu: (layer, be[j], 0, 0)),
                      pl.BlockSpec((1, 1, f, d), lambda j, be, nu: (layer, be[j], 0, 0))],
            out_specs=pl.BlockSpec((EXPERT_ROWS, d), lambda j, be, nu: (j, 0)),
            scratch_shapes=[pltpu.VMEM((d, f), BF16), pltpu.VMEM((d, f), BF16), pltpu.VMEM((f, d), BF16)]),
        compiler_params=_cparams("arbitrary"),
        name="experts",
    )(block_e, n_used, xs, w_gate, w_up, w_down)


def _combine_kernel(dest_ref, dest_next_ref, wt_ref, h_ref, ple_ref, ys_ref, g_ref, b_ref, o_ref, buf, sem):
    i = pl.program_id(0)
    nt = pl.num_programs(0)
    tm = h_ref.shape[0]
    half = i % 2

    def issue_tile(idx_ref, slot):
        def issue(it, c):
            for u in range(ROW_DMA_UNROLL):
                t = it * ROW_DMA_UNROLL + u
                for s in range(TOP_K):
                    pltpu.make_async_copy(ys_ref.at[pl.ds(idx_ref[0, s, t], 1)],
                                          buf.at[slot, s, pl.ds(t, 1)], sem.at[slot]).start()
            return c

        lax.fori_loop(0, tm // ROW_DMA_UNROLL, issue, 0)

    @pl.when(i == 0)
    def _():
        issue_tile(dest_ref, 0)

    @pl.when(i + 1 < nt)
    def _():
        issue_tile(dest_next_ref, 1 - half)

    for s in range(TOP_K):
        pltpu.make_async_copy(ys_ref.at[pl.ds(0, tm)], buf.at[half, s], sem.at[half]).wait()

    w_cols = jnp.concatenate([wt_ref[0], jnp.zeros((LANES - TOP_K, tm), F32)], axis=0).T
    ffn = buf[half, 0] * w_cols[:, 0:1] + buf[half, 1] * w_cols[:, 1:2]
    o_ref[...] = _layer_norm(DN_ALPHA * h_ref[...] + ffn + ple_ref[...], g_ref[...], b_ref[...])


def _combine_call(dest, wt, h, ple, ys, ln_g, ln_b):
    m, d = h.shape
    nt, _, tm = dest.shape
    assert tm % ROW_DMA_UNROLL == 0
    tok = lambda **kw: pl.BlockSpec((1, TOP_K, tm), lambda i: (i, 0, 0), **kw)
    tok_next = pl.BlockSpec((1, TOP_K, tm), lambda i: (jnp.minimum(i + 1, nt - 1), 0, 0),
                            memory_space=pltpu.SMEM)
    rows = pl.BlockSpec((tm, d), lambda i: (i, 0))
    vec = pl.BlockSpec((1, d), lambda i: (0, 0))
    return pl.pallas_call(
        _combine_kernel,
        out_shape=jax.ShapeDtypeStruct((m, d), F32),
        grid=(nt,),
        in_specs=[tok(memory_space=pltpu.SMEM), tok_next, tok(), rows, rows,
                  pl.BlockSpec(memory_space=pl.ANY), vec, vec],
        out_specs=rows,
        scratch_shapes=[pltpu.VMEM((2, TOP_K, tm, d), F32), pltpu.SemaphoreType.DMA((2,))],
        compiler_params=_cparams("arbitrary"),
        name="combine",
    )(dest, dest, wt, h, ple, ys, ln_g.reshape(1, d), ln_b.reshape(1, d))


SEQS_PER_STEP = SUBLANES


def _gdn_sample_kernel(new_ref, prev_ref, z_ref, ab_ref, w_ref, alog_ref, dtb_ref, ng_ref, s_ref,
                       y_ref, so_ref):
    w = w_ref[...]
    pre = prev_ref[0] * w[0:1]
    for j in range(1, GDN_CONV - 1):
        pre = pre + prev_ref[j] * w[j:j + 1]
    qkv = _silu(pre + new_ref[...] * w[GDN_CONV - 1:GDN_CONV])
    g_all, beta_all = _gdn_gates(ab_ref[...], alog_ref[...], dtb_ref[...])
    ng = ng_ref[...]
    z = z_ref[...]
    pad = jnp.zeros((LANES - SEQS_PER_STEP, GDN_DK), F32)
    for h in range(GDN_HEADS):
        q = _l2_normalize(qkv[:, h * GDN_DK:(h + 1) * GDN_DK]) * (GDN_DK ** -0.5)
        k = _l2_normalize(qkv[:, GDN_QK + h * GDN_DK:GDN_QK + (h + 1) * GDN_DK])
        v = qkv[:, 2 * GDN_QK + h * GDN_DV:2 * GDN_QK + (h + 1) * GDN_DV]
        eg = jnp.exp(g_all[:, h:h + 1])
        beta = beta_all[:, GDN_HEADS + h:GDN_HEADS + h + 1]
        qk = jnp.sum(q * k, axis=-1, keepdims=True)
        k_t = jnp.concatenate([k, pad], axis=0).T
        qg_t = jnp.concatenate([q * eg, pad], axis=0).T
        outs = []
        for i in range(SEQS_PER_STEP):
            s_mat = s_ref[0, i, h]
            k_col = k_t[:, i:i + 1]
            k_s = jnp.sum(k_col * s_mat, axis=0, keepdims=True)
            q_s = jnp.sum(qg_t[:, i:i + 1] * s_mat, axis=0, keepdims=True)
            v_new = beta[i:i + 1] * (v[i:i + 1] - eg[i:i + 1] * k_s)
            outs.append(q_s + qk[i:i + 1] * v_new)
            so_ref[i, h] = s_mat * eg[i:i + 1] + k_col * v_new
        o = jnp.concatenate(outs, axis=0)
        o = o * lax.rsqrt(jnp.mean(o * o, axis=-1, keepdims=True) + NORM_EPS) * ng
        y_ref[:, h * GDN_DV:(h + 1) * GDN_DV] = (o * _silu(z[:, h * GDN_DV:(h + 1) * GDN_DV])).astype(y_ref.dtype)


def _gdn_sample_call(conv_new, conv_prev_t, z, ab, conv_w, a_log, dt_bias, norm_g, state, layer):
    n, c = conv_new.shape
    sp = SEQS_PER_STEP
    alog_row = jnp.pad(a_log, (0, LANES - GDN_HEADS)).reshape(1, LANES)
    dtb_row = jnp.pad(dt_bias, (0, LANES - GDN_HEADS)).reshape(1, LANES)
    row_spec = pl.BlockSpec((1, LANES), lambda i: (0, 0))
    s_spec = pl.BlockSpec((sp, GDN_HEADS, GDN_DK, GDN_DV), lambda i: (i, 0, 0, 0))
    s_in_spec = pl.BlockSpec((1, sp, GDN_HEADS, GDN_DK, GDN_DV), lambda i: (layer, i, 0, 0, 0))
    return pl.pallas_call(
        _gdn_sample_kernel,
        out_shape=(jax.ShapeDtypeStruct((n, GDN_VW), BF16), jax.ShapeDtypeStruct(state.shape[1:], F32)),
        grid=(n // sp,),
        in_specs=[pl.BlockSpec((sp, c), lambda i: (i, 0)),
                  pl.BlockSpec((GDN_CONV - 1, sp, c), lambda i: (0, i, 0)),
                  pl.BlockSpec((sp, GDN_VW), lambda i: (i, 0)),
                  pl.BlockSpec((sp, LANES), lambda i: (i, 0)),
                  pl.BlockSpec((GDN_CONV, c), lambda i: (0, 0)),
                  row_spec, row_spec, row_spec, s_in_spec],
        out_specs=(pl.BlockSpec((sp, GDN_VW), lambda i: (i, 0)), s_spec),
        compiler_params=_cparams("parallel"),
        name="gdn_sample",
    )(conv_new, conv_prev_t, z, ab, conv_w, alog_row, dtb_row, norm_g.reshape(1, GDN_DV), state)


PAGES_PER_SEQ = PAST_LEN // PAGE_SIZE
PAGES_PER_BLOCK = MOBA_BLOCK // PAGE_SIZE
PAST_BLOCKS = PAST_LEN // MOBA_BLOCK


def _moba_sample_kernel(pt_ref, q_ref, qc_ref, kn_ref, vn_ref, rb_ref, *rest):
    del pt_ref
    np_ = PAGES_PER_SEQ
    k_pages = rest[:np_]
    v_pages = rest[np_:2 * np_]
    o_ref, bias_s = rest[2 * np_:]
    n = pl.program_id(0)
    nh, hd = MOBA_HEADS, MOBA_HD
    rb = rb_ref[...]
    lane_pos = lax.broadcasted_iota(I32, (nh, PAST_LEN), 1)

    @pl.when(n == 0)
    def _():
        dist = PAST_LEN - lane_pos
        b = jnp.broadcast_to(rb[:, 0:1], (nh, PAST_LEN))
        for kk in range(1, REL_BUCKETS):
            b = jnp.where(dist >= T5_THRESHOLDS[kk], rb[:, kk:kk + 1], b)
        bias_s[...] = b

    q = q_ref[0]
    q_cols = [jnp.broadcast_to(qc_ref[0, h], (hd, PAGE_SIZE)) for h in range(nh)]

    page_rows = []
    for j in range(np_):
        rows = [jnp.sum(k_pages[j][0, 0, h] * q_cols[h], axis=0, keepdims=True) for h in range(nh)]
        page_rows.append(jnp.concatenate(rows, axis=0))
    s_raw = jnp.concatenate(page_rows, axis=1)

    gates = [jnp.sum(s_raw[:, b * MOBA_BLOCK:(b + 1) * MOBA_BLOCK], axis=1, keepdims=True) * (1.0 / MOBA_BLOCK)
             for b in range(PAST_BLOCKS)]
    sel = jnp.zeros((nh, PAST_LEN), F32)
    for b in range(PAST_BLOCKS):
        rank = jnp.zeros((nh, 1), F32)
        for b2 in range(PAST_BLOCKS):
            if b2 != b:
                ahead = (gates[b2] > gates[b]) | (gates[b2] == gates[b]) if b2 < b else gates[b2] > gates[b]
                rank = rank + jnp.where(ahead, 1.0, 0.0)
        sel = jnp.where(lane_pos // MOBA_BLOCK == b, jnp.where(rank < MOBA_TOPK, 1.0, 0.0), sel)

    scale = hd ** -0.5
    s = jnp.where(sel > 0.0, s_raw * scale + bias_s[...], NEG_BIG)
    s_new = jnp.sum(q * kn_ref[0], axis=1, keepdims=True) * scale + rb[:, 0:1]
    m = jnp.maximum(jnp.max(s, axis=1, keepdims=True), s_new)
    p = jnp.exp(s - m)
    p_new = jnp.exp(s_new - m)
    inv_l = 1.0 / (jnp.sum(p, axis=1, keepdims=True) + p_new)
    pn = p * inv_l

    ones = jnp.ones((SUBLANES, PAGE_SIZE), F32)
    out_rows = []
    for h in range(nh):
        acc = jnp.zeros((hd, PAGE_SIZE), F32)
        for j in range(np_):
            acc = acc + pn[h:h + 1, j * PAGE_SIZE:(j + 1) * PAGE_SIZE] * v_pages[j][0, 0, h]
        out_rows.append(_dot_nt(ones, acc, precision=HIGHEST)[0:1])
    o_ref[0] = (jnp.concatenate(out_rows, axis=0) + (p_new * inv_l) * vn_ref[0]).astype(o_ref.dtype)


def _moba_sample_call(q, k_new, v_new, cache_k, cache_v, layer, page_table, rel_bias):
    n, nh, hd = q.shape
    np_ = PAGES_PER_SEQ
    rb = jnp.pad(rel_bias.T, ((0, 0), (0, LANES - REL_BUCKETS)))
    ck_t = cache_k.transpose(0, 1, 3, 4, 2)
    cv_t = cache_v.transpose(0, 1, 3, 4, 2)
    tok = pl.BlockSpec((1, nh, hd), lambda i, pt: (i, 0, 0))
    page = lambda j: pl.BlockSpec((1, 1, nh, hd, PAGE_SIZE),
                                  lambda i, pt: (layer, pt[i * np_ + j], 0, 0, 0))
    return pl.pallas_call(
        _moba_sample_kernel,
        out_shape=jax.ShapeDtypeStruct((n, nh, hd), BF16),
        grid_spec=pltpu.PrefetchScalarGridSpec(
            num_scalar_prefetch=1,
            grid=(n,),
            in_specs=[tok, pl.BlockSpec((1, nh, hd, 1), lambda i, pt: (i, 0, 0, 0)), tok, tok,
                      pl.BlockSpec((nh, LANES), lambda i, pt: (0, 0))]
            + [page(j) for j in range(np_)] + [page(j) for j in range(np_)],
            out_specs=tok,
            scratch_shapes=[pltpu.VMEM((nh, PAST_LEN), F32)]),
        compiler_params=_cparams("arbitrary"),
        name="moba_sample",
    )(page_table.reshape(-1), q, q.reshape(n, nh, hd, 1), k_new, v_new, rb,
      *([ck_t] * np_), *([cv_t] * np_))


def _moe_layout(counts_f32, n_rows):
    counts = counts_f32.astype(I32)
    padded = (counts + EXPERT_ROWS - 1) // EXPERT_ROWS * EXPERT_ROWS
    ends = jnp.cumsum(padded)
    pstart = ends - padded
    nb = n_rows // EXPERT_ROWS
    block_start = jnp.arange(nb, dtype=I32) * EXPERT_ROWS
    block_e = jnp.minimum(jnp.sum((ends[None, :] <= block_start[:, None]).astype(I32), axis=1), N_EXPERTS - 1)
    n_used = (ends[-1:] // EXPERT_ROWS).astype(I32)
    return pstart.astype(I32), block_e.astype(I32), n_used


def _moe_rows(m):
    n_asg = m * TOP_K
    return -(-(n_asg + N_EXPERTS * (EXPERT_ROWS - 1)) // EXPERT_ROWS) * EXPERT_ROWS


MOE_TOKEN_TILE = 128


def _merge_group(ya, yb, gate, x, p, layer, lw):
    return _merge_call(ya, yb, gate, x, p, layer, lw['wug'], lw['wum'], lw['wo'], lw['wpg'], lw['wpp'],
                       lw['wrt'], lw['ln1_g'], lw['ln1_b'], _row_tile(x.shape[0], 2 * MERGE_SUB_ROWS))


def _moe_and_norm(groups, layer, lw):
    tn = MOE_TOKEN_TILE
    aff_t = jnp.concatenate([g[2] for g in groups], axis=1)
    eid, wt, pos, cnt = _route_call(aff_t, lw['b_router'], tn)
    n_rows = _moe_rows(aff_t.shape[1])
    pstart, block_e, n_used = _moe_layout(cnt[:, 0], n_rows)
    dest = _dest_call(pstart, eid, pos)
    tiles, t0 = [], 0
    for h, _, _ in groups:
        tiles.append(slice(t0, t0 + h.shape[0] // tn))
        t0 = tiles[-1].stop
    xs = jnp.zeros((n_rows, groups[0][0].shape[1]), F32)
    for (h, _, _), tl in zip(groups, tiles):
        xs = _dispatch_call(dest[tl], h, xs)
    ys = _expert_call(block_e, n_used, xs, lw['w_exp_gate'], lw['w_exp_up'], lw['w_exp_down'], layer)
    return [_combine_call(dest[tl], wt[tl], h, ple, ys, lw['ln2_g'], lw['ln2_b'])
            for (h, ple, _), tl in zip(groups, tiles)]


def kernel(x_prompt, x_sample, cache_k, cache_v, state_gdn, state_conv, page_table, p_prompt, p_sample,
           ln0_g, ln0_b, w_in, gdn_conv_w, gdn_a_log, gdn_dt_bias, gdn_norm_g, w_up_gdn, w_up_moba, w_o,
           ln1_g, ln1_b, rel_bias, w_router, b_router, w_exp_gate, w_exp_up, w_exp_down,
           w_ple_gate, w_ple_proj, ln2_g, ln2_b):
    assert GDN_DK == LANES and GDN_DV == LANES
    n_p, t_p, d = x_prompt.shape
    n_s, t_s, _ = x_sample.shape
    assert t_s == 1 and t_p % MOBA_BLOCK == 0 and n_s % SEQS_PER_STEP == 0
    m_p, m_s = n_p * t_p, n_s
    assert m_p % MOE_TOKEN_TILE == 0 and m_s % MOE_TOKEN_TILE == 0
    pp =p_prompt.reshape(DEPTH, m_p, PLE_DIM)
    ps = p_sample.reshape(DEPTH, m_s, PLE_DIM)
    heads = lambda a: a.reshape(a.shape[0], MOBA_HEADS, MOBA_HD)

    xp = _ln_call(x_prompt.reshape(m_p, d), ln0_g, ln0_b)
    xs = _ln_call(x_sample.reshape(m_s, d), ln0_g, ln0_b)
    wrt = w_router.T
    outs = {name: [] for name in ('kp', 'vp', 'sp', 'cp', 'ks', 'vs', 'ss', 'cs')}
    for i in range(DEPTH):
        w_parts = _split_w_in(w_in[i])
        lw = dict(wug=w_up_gdn[i].astype(BF16), wum=w_up_moba[i].astype(BF16), wo=w_o[i].astype(BF16),
                  wpg=w_ple_gate[i].astype(BF16), wpp=w_ple_proj[i].astype(BF16), wrt=wrt,
                  ln1_g=ln1_g[i], ln1_b=ln1_b[i], ln2_g=ln2_g[i], ln2_b=ln2_b[i], b_router=b_router,
                  w_exp_gate=w_exp_gate, w_exp_up=w_exp_up, w_exp_down=w_exp_down)

        conv, z, ab, q, gate, kt, vt = _proj_call(xp, w_parts, n_p)
        seq = lambda a: a.reshape(n_p, t_p, a.shape[-1])
        ya, s_new = _gdn_prompt_call(seq(conv), seq(z), seq(ab), gdn_conv_w[i], gdn_a_log[i],
                                     gdn_dt_bias[i], gdn_norm_g[i])
        yb = _moba_prompt_call(seq(q), kt, vt, rel_bias)
        group_p = _merge_group(ya.reshape(m_p, GDN_VW), yb.reshape(m_p, MOBA_W), gate, xp, pp, i, lw)
        untranspose = lambda a: a.reshape(n_p, MOBA_HEADS, MOBA_HD, t_p).transpose(0, 3, 1, 2)
        outs['kp'].append(untranspose(kt))
        outs['vp'].append(untranspose(vt))
        outs['sp'].append(s_new)
        outs['cp'].append(seq(conv)[:, t_p - (GDN_CONV - 1):, :])

        conv, z, ab, q, gate, kt, vt = _proj_call(xs, w_parts, 1)
        k = kt[0].T
        v = vt[0].T
        ya, s_new = _gdn_sample_call(conv, state_conv[i].transpose(1, 0, 2), z, ab, gdn_conv_w[i],
                                     gdn_a_log[i], gdn_dt_bias[i], gdn_norm_g[i], state_gdn, i)
        yb = _moba_sample_call(heads(q), heads(k), heads(v), cache_k, cache_v, i, page_table, rel_bias)
        group_s = _merge_group(ya, yb.reshape(m_s, MOBA_W), gate, xs, ps, i, lw)
        xp, xs = _moe_and_norm([group_p, group_s], i, lw)
        outs['ks'].append(k.reshape(n_s, 1, MOBA_HEADS, MOBA_HD))
        outs['vs'].append(v.reshape(n_s, 1, MOBA_HEADS, MOBA_HD))
        outs['ss'].append(s_new)
        outs['cs'].append(jnp.concatenate([state_conv[i][:, 1:, :], conv[:, None, :]], axis=1))

    st = lambda name: jnp.stack(outs[name])
    return (xp.reshape(n_p, t_p, d), xs.reshape(n_s, 1, d), st('kp'), st('vp'), st('sp'), st('cp'),
            st('ks'), st('vs'), st('ss'), st('cs'))
```
